```python
import math
import jax, jax.numpy as jnp
from jax import lax
import numpy as np

D_MODEL = 1024
BATCH = 4
SEQ = 4096
DEPTH = 1
DEC_BATCH = 128
DEC_SEQ = 4
PAST_LEN = 8192
PAGE_SIZE = 128

HEAD_DIM = 64
N_FOX_HEADS = 8
N_RET_HEADS = 8
FOX_WIDTH = N_FOX_HEADS * HEAD_DIM
RET_WIDTH = N_RET_HEADS * HEAD_DIM
MIX_WIDTH = FOX_WIDTH + RET_WIDTH
IN_WIDTH = 3 * FOX_WIDTH + N_FOX_HEADS + 4 * RET_WIDTH
SPLIT_POINTS = (FOX_WIDTH, 2 * FOX_WIDTH, 3 * FOX_WIDTH,
                3 * FOX_WIDTH + N_FOX_HEADS,
                3 * FOX_WIDTH + N_FOX_HEADS + RET_WIDTH,
                3 * FOX_WIDTH + N_FOX_HEADS + 2 * RET_WIDTH,
                3 * FOX_WIDTH + N_FOX_HEADS + 3 * RET_WIDTH)
Q_BLOCK = 128
RET_CHUNK = 128
N_MEM = 256
N_MEM_HEADS = 4
MEM_HEAD_DIM = 64
MEM_WIDTH = N_MEM_HEADS * MEM_HEAD_DIM
D_FF = 2816
ROPE_THETA = 10000.0
EPS = 1e-6
FFN_RES = 0.5
FORGET_BIAS = 8.0

kernel_name = "fox_retention_parallel_heads_decoder_step"


def rmsnorm(x, g):
    xf = x.astype(jnp.float32)
    y = xf * lax.rsqrt(jnp.mean(xf * xf, axis=-1, keepdims=True) + EPS)
    return (y * g.astype(jnp.float32)).astype(x.dtype)


def swiglu(h, w_gate, w_up, w_down):
    return (jax.nn.silu(h @ w_gate) * (h @ w_up)) @ w_down


def rope(x, pos):
    half = x.shape[-1] // 2
    inv = ROPE_THETA ** (-jnp.arange(half, dtype=jnp.float32) / half)
    ang = pos.astype(jnp.float32)[:, None] * inv[None, :]
    cos = jnp.cos(ang)[None, :, None, :]
    sin = jnp.sin(ang)[None, :, None, :]
    xf = x.astype(jnp.float32)
    x1, x2 = xf[..., :half], xf[..., half:]
    return jnp.concatenate([x1 * cos - x2 * sin, x1 * sin + x2 * cos], axis=-1).astype(x.dtype)


def retention_log_gamma():
    return jnp.log(1.0 - 2.0 ** (-5.0 - jnp.arange(N_RET_HEADS, dtype=jnp.float32)))


def project_heads(h, w_in, b_f, pos):
    proj = h @ w_in
    q_a, k_a, v_a, f_a, q_r, k_r, v_r, g_r = jnp.split(proj, SPLIT_POINTS, axis=-1)
    heads = lambda t: t.reshape(t.shape[0], t.shape[1], -1, HEAD_DIM)
    log_f = jax.nn.log_sigmoid(f_a.astype(jnp.float32) + b_f.astype(jnp.float32))
    q_r = rope(heads(q_r), pos)
    k_r = rope(heads(k_r), pos) * HEAD_DIM ** -0.5
    return heads(q_a), heads(k_a), heads(v_a), log_f, q_r, k_r, heads(v_r), g_r


def fox_prompt(q, k, v, log_f):
    b, s, h, d = q.shape
    cum_f = lax.cumsum(log_f, axis=1)
    nb = s // Q_BLOCK
    q_blocks = q.reshape(b, nb, Q_BLOCK, h, d).transpose(1, 0, 2, 3, 4)
    f_blocks = cum_f.reshape(b, nb, Q_BLOCK, h).transpose(1, 0, 2, 3)
    key_pos = jnp.arange(s)
    f_keys = cum_f.transpose(0, 2, 1)[:, :, None, :]

    def block(args):
        i, q_i, f_i = args
        sc = jnp.einsum('bqhd,bkhd->bhqk', q_i, k, preferred_element_type=jnp.float32) * d ** -0.5
        sc = sc + f_i.transpose(0, 2, 1)[..., None] - f_keys
        q_pos = i * Q_BLOCK + jnp.arange(Q_BLOCK)
        sc = jnp.where(key_pos[None, :] <= q_pos[:, None], sc, -jnp.inf)
        p = jax.nn.softmax(sc, axis=-1).astype(v.dtype)
        return jnp.einsum('bhqk,bkhd->bqhd', p, v)

    out = lax.map(block, (jnp.arange(nb), q_blocks, f_blocks))
    return out.transpose(1, 0, 2, 3, 4).reshape(b, s, h, d)


def fox_sample(q, k_new, v_new, log_f_new, k_past, v_past, log_f_past):
    d = q.shape[-1]
    p_len = k_past.shape[1]
    l_new = q.shape[1]
    cum_new = lax.cumsum(log_f_new, axis=1)
    past_decay = lax.cumsum(log_f_past, axis=1, reverse=True) - log_f_past
    f_q = cum_new.transpose(0, 2, 1)[..., None]
    s_past = jnp.einsum('bqhd,bkhd->bhqk', q, k_past, preferred_element_type=jnp.float32) * d ** -0.5
    s_past = s_past + f_q + past_decay.transpose(0, 2, 1)[:, :, None, :]
    s_new = jnp.einsum('bqhd,bkhd->bhqk', q, k_new, preferred_element_type=jnp.float32) * d ** -0.5
    s_new = s_new + f_q - cum_new.transpose(0, 2, 1)[:, :, None, :]
    causal = jnp.tril(jnp.ones((l_new, l_new), dtype=bool))
    s_new = jnp.where(causal, s_new, -jnp.inf)
    p = jax.nn.softmax(jnp.concatenate([s_past, s_new], axis=-1), axis=-1).astype(v_new.dtype)
    return (jnp.einsum('bhqk,bkhd->bqhd', p[..., :p_len], v_past)
            + jnp.einsum('bhqk,bkhd->bqhd', p[..., p_len:], v_new))


def retention_chunk(state, q, k, v, log_gamma):
    l = q.shape[1]
    q, k, v = (t.astype(jnp.float32) for t in (q, k, v))
    idx = jnp.arange(l, dtype=jnp.float32)
    diff = idx[:, None] - idx[None, :]
    decay = jnp.where(diff >= 0, jnp.exp(jnp.maximum(diff, 0.0)[None] * log_gamma[:, None, None]), 0.0)
    scores = jnp.einsum('blhd,bmhd->bhlm', q, k) * decay[None]
    o_inner = jnp.einsum('bhlm,bmhe->blhe', scores, v)
    cross_decay = jnp.exp((idx + 1.0)[:, None] * log_gamma[None, :])
    o_cross = jnp.einsum('blhd,bhde->blhe', q, state) * cross_decay[None, :, :, None]
    k_decay = jnp.exp((l - 1.0 - idx)[:, None] * log_gamma[None, :])
    new_state = (jnp.exp(l * log_gamma)[None, :, None, None] * state
                 + jnp.einsum('blhd,blhe->bhde', k * k_decay[None, :, :, None], v))
    return o_inner + o_cross, new_state


def retention_prompt(q, k, v, log_gamma):
    b, s, h, d = q.shape
    nc = s // RET_CHUNK
    to_chunks = lambda t: t.reshape(b, nc, RET_CHUNK, h, t.shape[-1]).transpose(1, 0, 2, 3, 4)
    state0 = jnp.zeros((b, h, d, v.shape[-1]), jnp.float32)

    def step(state, qkv):
        o, state = retention_chunk(state, qkv[0], qkv[1], qkv[2], log_gamma)
        return state, o

    state, o = lax.scan(step, state0, (to_chunks(q), to_chunks(k), to_chunks(v)))
    return o.transpose(1, 0, 2, 3, 4).reshape(b, s, h, -1), state


def merge_heads(o_fox, o_ret, gate, g_ret, w_out):
    b, l = o_fox.shape[:2]
    of = o_ret.astype(jnp.float32)
    of = of * lax.rsqrt(jnp.mean(of * of, axis=-1, keepdims=True) + EPS)
    o_r = (of.reshape(b, l, RET_WIDTH) * g_ret.astype(jnp.float32)).astype(gate.dtype) * jax.nn.silu(gate)
    cat = jnp.concatenate([o_fox.reshape(b, l, FOX_WIDTH).astype(gate.dtype), o_r], axis=-1)
    return cat @ w_out


def memory_kv(mem, g_mem, w_ck, w_cv):
    m = rmsnorm(mem, g_mem)
    b, n = mem.shape[:2]
    return ((m @ w_ck).reshape(b, n, N_MEM_HEADS, MEM_HEAD_DIM),
            (m @ w_cv).reshape(b, n, N_MEM_HEADS, MEM_HEAD_DIM))


def cross_attend(h, mem_k, mem_v, w_cq, w_co):
    b, l = h.shape[:2]
    q = (h @ w_cq).reshape(b, l, N_MEM_HEADS, MEM_HEAD_DIM)
    sc = jnp.einsum('bqhd,bkhd->bhqk', q, mem_k, preferred_element_type=jnp.float32) * MEM_HEAD_DIM ** -0.5
    p = jax.nn.softmax(sc, axis=-1).astype(mem_v.dtype)
    o = jnp.einsum('bhqk,bkhd->bqhd', p, mem_v).reshape(b, l, MEM_WIDTH)
    return o @ w_co


def setup_inputs(seed: int = 0) -> dict:
    key = jax.random.key(seed)
    keys = iter(jax.random.split(key, 48))
    nk = lambda: next(keys)
    f32 = jnp.float32
    n_pages = PAST_LEN // PAGE_SIZE
    n_used = DEC_BATCH * n_pages
    n_phys = n_used + n_used // 4

    def w(shape, fan_in):
        return jax.random.normal(nk(), shape, f32) * fan_in ** -0.5

    def gain(shape):
        return 1.0 + 0.05 * jax.random.normal(nk(), shape, f32)

    x_prompt = jax.random.normal(nk(), (BATCH, SEQ, D_MODEL), f32)
    x_sample = jax.random.normal(nk(), (DEC_BATCH, DEC_SEQ, D_MODEL), f32)
    mem_prompt = jax.random.normal(nk(), (BATCH, N_MEM, D_MODEL), f32)
    cache_fox_k = jax.random.normal(nk(), (DEPTH, n_phys, PAGE_SIZE, N_FOX_HEADS, HEAD_DIM), f32)
    cache_fox_v = jax.random.normal(nk(), (DEPTH, n_phys, PAGE_SIZE, N_FOX_HEADS, HEAD_DIM), f32)
    cache_fox_logf = jax.nn.log_sigmoid(
        FORGET_BIAS + 0.5 * jax.random.normal(nk(), (DEPTH, n_phys, PAGE_SIZE, N_FOX_HEADS), f32))
    state_ret = 0.5 * jax.random.normal(nk(), (DEPTH, DEC_BATCH, N_RET_HEADS, HEAD_DIM, HEAD_DIM), f32)
    cache_mem_k = jax.random.normal(nk(), (DEPTH, DEC_BATCH, N_MEM, N_MEM_HEADS, MEM_HEAD_DIM), f32)
    cache_mem_v = jax.random.normal(nk(), (DEPTH, DEC_BATCH, N_MEM, N_MEM_HEADS, MEM_HEAD_DIM), f32)
    page_table = jax.random.permutation(nk(), n_phys)[:n_used].reshape(DEC_BATCH, n_pages).astype(jnp.int32)

    return {
        "x_prompt": x_prompt, "x_sample": x_sample, "mem_prompt": mem_prompt,
        "cache_fox_k": cache_fox_k, "cache_fox_v": cache_fox_v, "cache_fox_logf": cache_fox_logf,
        "state_ret": state_ret, "cache_mem_k": cache_mem_k, "cache_mem_v": cache_mem_v,
        "page_table": page_table,
        "g_ffn1": gain((DEPTH, D_MODEL)),
        "w1_gate": w((DEPTH, D_MODEL, D_FF), D_MODEL),
        "w1_up": w((DEPTH, D_MODEL, D_FF), D_MODEL),
        "w1_down": w((DEPTH, D_FF, D_MODEL), D_FF),
        "g_mix": gain((DEPTH, D_MODEL)),
        "w_in": w((DEPTH, D_MODEL, IN_WIDTH), D_MODEL),
        "b_f": FORGET_BIAS + 0.5 * jax.random.normal(nk(), (DEPTH, N_FOX_HEADS), f32),
        "g_ret": gain((DEPTH, RET_WIDTH)),
        "w_out": w((DEPTH, MIX_WIDTH, D_MODEL), MIX_WIDTH),
        "g_cross": gain((DEPTH, D_MODEL)),
        "g_mem": gain((DEPTH, D_MODEL)),
        "w_cq": w((DEPTH, D_MODEL, MEM_WIDTH), D_MODEL),
        "w_ck": w((DEPTH, D_MODEL, MEM_WIDTH), D_MODEL),
        "w_cv": w((DEPTH, D_MODEL, MEM_WIDTH), D_MODEL),
        "w_co": w((DEPTH, MEM_WIDTH, D_MODEL), MEM_WIDTH),
        "g_ffn2": gain((DEPTH, D_MODEL)),
        "w2_gate": w((DEPTH, D_MODEL, D_FF), D_MODEL),
        "w2_up": w((DEPTH, D_MODEL, D_FF), D_MODEL),
        "w2_down": w((DEPTH, D_FF, D_MODEL), D_FF),
        "g_final": gain((D_MODEL,)),
    }


def reference(x_prompt, x_sample, mem_prompt, cache_fox_k, cache_fox_v, cache_fox_logf,
              state_ret, cache_mem_k, cache_mem_v, page_table,
              g_ffn1, w1_gate, w1_up, w1_down, g_mix, w_in, b_f, g_ret, w_out,
              g_cross, g_mem, w_cq, w_ck, w_cv, w_co,
              g_ffn2, w2_gate, w2_up, w2_down, g_final):
    log_gamma = retention_log_gamma()
    seq = x_prompt.shape[1]
    dec_batch, dec_seq = x_sample.shape[0], x_sample.shape[1]
    past_len = page_table.shape[1] * cache_fox_k.shape[2]
    pos_p = jnp.arange(seq, dtype=jnp.int32)
    pos_s = past_len + jnp.arange(dec_seq, dtype=jnp.int32)
    xp, xs = x_prompt, x_sample
    p_k, p_v, p_f, p_st, p_mk, p_mv, s_k, s_v, s_f, s_st = ([] for _ in range(10))
    for l in range(DEPTH):
        xp = xp + FFN_RES * swiglu(rmsnorm(xp, g_ffn1[l]), w1_gate[l], w1_up[l], w1_down[l])
        xs = xs + FFN_RES * swiglu(rmsnorm(xs, g_ffn1[l]), w1_gate[l], w1_up[l], w1_down[l])

        qa, ka, va, lf, qr, kr, vr, gr = project_heads(rmsnorm(xp, g_mix[l]), w_in[l], b_f[l], pos_p)
        o_fox = fox_prompt(qa, ka, va, lf)
        o_ret, st_p = retention_prompt(qr, kr, vr, log_gamma)
        xp = xp + merge_heads(o_fox, o_ret, gr, g_ret[l], w_out[l])
        p_k.append(ka)
        p_v.append(va)
        p_f.append(lf)
        p_st.append(st_p)

        qa_s, ka_s, va_s, lf_s, qr_s, kr_s, vr_s, gr_s = project_heads(rmsnorm(xs, g_mix[l]), w_in[l], b_f[l], pos_s)
        k_past = cache_fox_k[l][page_table].reshape(dec_batch, -1, N_FOX_HEADS, HEAD_DIM)
        v_past = cache_fox_v[l][page_table].reshape(dec_batch, -1, N_FOX_HEADS, HEAD_DIM)
        lf_past = cache_fox_logf[l][page_table].reshape(dec_batch, -1, N_FOX_HEADS).astype(jnp.float32)
        o_fox_s = fox_sample(qa_s, ka_s, va_s, lf_s, k_past, v_past, lf_past)
        o_ret_s, st_s = retention_chunk(state_ret[l].astype(jnp.float32), qr_s, kr_s, vr_s, log_gamma)
        xs = xs + merge_heads(o_fox_s, o_ret_s, gr_s, g_ret[l], w_out[l])
        s_k.append(ka_s)
        s_v.append(va_s)
        s_f.append(lf_s)
        s_st.append(st_s)

        mk, mv = memory_kv(mem_prompt, g_mem[l], w_ck[l], w_cv[l])
        xp = xp + cross_attend(rmsnorm(xp, g_cross[l]), mk, mv, w_cq[l], w_co[l])
        xs = xs + cross_attend(rmsnorm(xs, g_cross[l]), cache_mem_k[l], cache_mem_v[l], w_cq[l], w_co[l])
        p_mk.append(mk)
        p_mv.append(mv)

        xp = xp + FFN_RES * swiglu(rmsnorm(xp, g_ffn2[l]), w2_gate[l], w2_up[l], w2_down[l])
        xs = xs + FFN_RES * swiglu(rmsnorm(xs, g_ffn2[l]), w2_gate[l], w2_up[l], w2_down[l])

    y_prompt = rmsnorm(xp, g_final)
    y_sample = rmsnorm(xs, g_final)
    return (y_prompt, y_sample,
            jnp.stack(p_k), jnp.stack(p_v), jnp.stack(p_f), jnp.stack(p_st), jnp.stack(p_mk), jnp.stack(p_mv),
            jnp.stack(s_k), jnp.stack(s_v), jnp.stack(s_f), jnp.stack(s_st))
```

```python
import functools

import jax
import jax.numpy as jnp
from jax import lax
from jax.experimental import pallas as pl
from jax.experimental.pallas import tpu as pltpu

F32 = jnp.float32
BF16 = jnp.bfloat16

HEAD_DIM = 64
N_HEADS = 8
GROUP_WIDTH = N_HEADS * HEAD_DIM
N_MEM_HEADS = 4
ROPE_THETA = 10000.0
EPS = 1e-6
FFN_RES = 0.5
Q_SCALE = HEAD_DIM ** -0.5
RET_CHUNK = 128
LANES = 128
VMEM_LIMIT = 56 * 1024 * 1024


def _params(*sem):
    return pltpu.CompilerParams(dimension_semantics=sem, vmem_limit_bytes=VMEM_LIMIT)


def _resident(shape):
    nd = len(shape)
    return pl.BlockSpec(shape, lambda *_: (0,) * nd, pipeline_mode=pl.Buffered(1))


def _dot(a, b):
    return jnp.dot(a, b, preferred_element_type=F32)


def _dot_nt(a, b):
    return lax.dot_general(a, b, (((1,), (1,)), ((), ())), preferred_element_type=F32)


def _rms(x, g):
    return x * lax.rsqrt(jnp.mean(x * x, axis=-1, keepdims=True) + EPS) * g


def _silu(x):
    return x * jax.nn.sigmoid(x)


def _split3(x):
    hi = x.astype(BF16)
    r1 = x - hi.astype(F32)
    mid = r1.astype(BF16)
    lo = (r1 - mid.astype(F32)).astype(BF16)
    return hi, mid, lo


def _dot_exact01(x, w01, w_left=False):
    mm = (lambda part: _dot(w01, part)) if w_left else (lambda part: _dot(part, w01))
    hi, mid, lo = _split3(x)
    return mm(hi) + mm(mid) + mm(lo)


def _ffn_kernel(x_ref, g_ref, wg_ref, wu_ref, wd_ref, gf_ref, o_ref, *, chunks, final_norm):
    x = x_ref[...]
    h = _rms(x, g_ref[...]).astype(BF16)
    acc = jnp.zeros_like(x)
    start = 0
    for width in chunks:
        sl = slice(start, start + width)
        a = _silu(_dot(h, wg_ref[:, sl])) * _dot(h, wu_ref[:, sl])
        acc = acc + _dot(a.astype(BF16), wd_ref[sl, :])
        start += width
    y = x + FFN_RES * acc
    if final_norm:
        y = _rms(y, gf_ref[...])
    o_ref[...] = y


def _ffn(x, g, wg, wu, wd, gf, *, final_norm, tm):
    t, d = x.shape
    d_ff = wg.shape[1]
    chunks = [512] * (d_ff // 512)
    if d_ff % 512:
        chunks.append(d_ff % 512)
    row = pl.BlockSpec((tm, d), lambda i: (i, 0))
    return pl.pallas_call(
        functools.partial(_ffn_kernel, chunks=tuple(chunks), final_norm=final_norm),
        grid=(t // tm,),
        in_specs=[row, _resident((1, d)), _resident(wg.shape), _resident(wu.shape),
                  _resident(wd.shape), _resident((1, d))],
        out_specs=row,
        out_shape=jax.ShapeDtypeStruct((t, d), F32),
        compiler_params=_params("parallel"),
        name="ffn",
    )(x, g, wg, wu, wd, gf)


def _inproj_kernel(x_ref, g_ref, wfox_ref, wf_ref, bf_ref, wret_ref, cos_ref, sin_ref,
                   qa_ref, ka_ref, va_ref, kab_ref, vab_ref, lf_ref, qr_ref, kr_ref, vr_ref, gr_ref):
    h = _rms(x_ref[...], g_ref[...]).astype(BF16)
    w = GROUP_WIDTH
    qa_ref[...] = _dot(h, wfox_ref[:, 0:w]) * Q_SCALE
    ka = _dot(h, wfox_ref[:, w:2 * w])
    ka_ref[...] = ka
    kab_ref[...] = ka.astype(BF16)
    va = _dot(h, wfox_ref[:, 2 * w:3 * w])
    va_ref[...] = va
    vab_ref[...] = va.astype(BF16)
    z = _dot(h, wf_ref[...]) + bf_ref[...]
    lf_ref[...] = (jnp.minimum(z, 0.0) - jnp.log1p(jnp.exp(-jnp.abs(z))))[:, :N_HEADS]
    cos = cos_ref[...]
    sin = sin_ref[...]
    qr_ref[...] = (_dot(h, wret_ref[:, 0:w]) * cos + _dot(h, wret_ref[:, w:2 * w]) * sin).astype(BF16)
    kr = _dot(h, wret_ref[:, 2 * w:3 * w]) * cos + _dot(h, wret_ref[:, 3 * w:4 * w]) * sin
    kr_ref[...] = (kr * Q_SCALE).astype(BF16)
    vr_ref[...] = _dot(h, wret_ref[:, 4 * w:5 * w]).astype(BF16)
    gr_ref[...] = _dot(h, wret_ref[:, 5 * w:6 * w])


def _inproj(x, g, wfox, wf, bf, wret, cos, sin, *, tm):
    t, d = x.shape
    w = GROUP_WIDTH
    n_pos = cos.shape[0] // tm
    row = lambda width: pl.BlockSpec((tm, width), lambda i: (i, 0))
    tab = pl.BlockSpec((tm, w), lambda i: (i % n_pos, 0))
    sds = lambda width, dt: jax.ShapeDtypeStruct((t, width), dt)
    return pl.pallas_call(
        _inproj_kernel,
        grid=(t // tm,),
        in_specs=[row(d), _resident((1, d)), _resident(wfox.shape), _resident(wf.shape),
                  _resident(bf.shape), _resident(wret.shape), tab, tab],
        out_specs=[row(w), row(w), row(w), row(w), row(w), row(N_HEADS), row(w), row(w), row(w), row(w)],
        out_shape=[sds(w, F32), sds(w, F32), sds(w, F32), sds(w, BF16), sds(w, BF16), sds(N_HEADS, F32),
                   sds(w, BF16), sds(w, BF16), sds(w, BF16), sds(w, F32)],
        compiler_params=_params("parallel"),
        name="inproj",
    )(x, g, wfox, wf, bf, wret, cos, sin)


def _cumsum_kernel(lf_ref, fcol_ref, frow_ref, carry_col, carry_row, *, tc):
    @pl.when(pl.program_id(1) == 0)
    def _():
        carry_col[...] = jnp.zeros_like(carry_col)
        carry_row[...] = jnp.zeros_like(carry_row)

    lf = lf_ref[...]
    r = lax.broadcasted_iota(jnp.int32, (tc, tc), 0)
    c = lax.broadcasted_iota(jnp.int32, (tc, tc), 1)
    lower = jnp.where(c <= r, 1.0, 0.0).astype(BF16)
    fcol = _dot_exact01(lf, lower, w_left=True) + carry_col[...]
    fcol_ref[...] = fcol
    carry_col[...] = fcol[tc - 1:tc, :]
    upper = jnp.where(r <= c, 1.0, 0.0).astype(BF16)
    frow = _dot_exact01(lf.T[:N_HEADS, :], upper) + carry_row[...]
    frow_ref[...] = frow
    carry_row[...] = frow[:, tc - 1:tc]


def _cumsum(lf_pad, *, tc):
    b, s, _ = lf_pad.shape
    return pl.pallas_call(
        functools.partial(_cumsum_kernel, tc=tc),
        grid=(b, s // tc),
        in_specs=[pl.BlockSpec((None, tc, LANES), lambda i, j: (i, j, 0))],
        out_specs=[pl.BlockSpec((None, tc, LANES), lambda i, j: (i, j, 0)),
                   pl.BlockSpec((None, N_HEADS, tc), lambda i, j: (i, 0, j))],
        out_shape=[jax.ShapeDtypeStruct((b, s, LANES), F32), jax.ShapeDtypeStruct((b, N_HEADS, s), F32)],
        scratch_shapes=[pltpu.VMEM((1, LANES), F32), pltpu.VMEM((N_HEADS, 1), F32)],
        compiler_params=_params("parallel", "arbitrary"),
        name="logf_cumsum",
    )(lf_pad)


def _fox_prompt_kernel(q_ref, k_ref, v_ref, fcol_ref, frow_ref, o_ref, *, tq):
    hp = pl.program_id(1)
    qi = pl.program_id(2)
    lane = lax.broadcasted_iota(jnp.int32, (tq, LANES), 1)
    lo = lane < HEAD_DIM
    q = q_ref[...]
    zero = jnp.zeros_like(q)
    q_h = (jnp.where(lo, q, zero), jnp.where(lo, zero, q))
    fcol = fcol_ref[...]
    fq = tuple(jnp.sum(jnp.where(lane == 2 * hp + h, fcol, 0.0), axis=1, keepdims=True) for h in (0, 1))

    def block(kj, carry, masked):
        start = pl.multiple_of(kj * tq, tq)
        kb = k_ref[pl.ds(start, tq), :]
        vb = v_ref[pl.ds(start, tq), :]
        new = []
        for h in (0, 1):
            m_old, l_old, acc = carry[h]
            fk = frow_ref[2 * hp + h, pl.ds(kj, 1), :]
            t = _dot_nt(q_h[h], kb) - fk
            if masked:
                r = lax.broadcasted_iota(jnp.int32, (tq, tq), 0)
                c = lax.broadcasted_iota(jnp.int32, (tq, tq), 1)
                t = jnp.where(c <= r, t, -jnp.inf)
            m_new = jnp.maximum(m_old, jnp.max(t, axis=1, keepdims=True) + fq[h])
            p = jnp.exp(t + (fq[h] - m_new))
            alpha = jnp.exp(m_old - m_new)
            new.append((m_new, alpha * l_old + jnp.sum(p, axis=1, keepdims=True),
                        alpha * acc + _dot(p.astype(BF16), vb)))
        return tuple(new)

    init = tuple((jnp.full((tq, 1), -1e30, F32), jnp.zeros((tq, 1), F32), jnp.zeros((tq, LANES), F32))
                 for _ in (0, 1))
    carry = lax.fori_loop(0, qi, lambda kj, c: block(kj, c, False), init)
    (_, l0, a0), (_, l1, a1) = block(qi, carry, True)
    o_ref[...] = jnp.where(lo, a0 / l0, a1 / l1).astype(o_ref.dtype)


def _fox_prompt(q, k, v, fcol, frow4, *, tq):
    b, s, w = q.shape
    pairs = w // LANES
    qspec = pl.BlockSpec((None, tq, LANES), lambda i, p, j: (i, j, p))
    kvspec = pl.BlockSpec((None, s, LANES), lambda i, p, j: (i, 0, p))
    return pl.pallas_call(
        functools.partial(_fox_prompt_kernel, tq=tq),
        grid=(b, pairs, s // tq),
        in_specs=[qspec, kvspec, kvspec,
                  pl.BlockSpec((None, tq, LANES), lambda i, p, j: (i, j, 0)),
                  pl.BlockSpec((None, N_HEADS, s // tq, tq), lambda i, p, j: (i, 0, 0, 0))],
        out_specs=qspec,
        out_shape=jax.ShapeDtypeStruct((b, s, w), BF16),
        compiler_params=_params("parallel", "parallel", "arbitrary"),
        name="fox_prompt",
    )(q, k, v, fcol, frow4)


def _ret_kernel(q_ref, k_ref, v_ref, gate_ref, gret_ref, st0_ref, dec_ref, cross_ref, kdec_ref, gam_ref,
                o_ref, st_ref, state, *, n_chunks):
    c = RET_CHUNK
    j = pl.program_id(1)

    @pl.when(j == 0)
    def _():
        state[...] = st0_ref[...]

    lane = lax.broadcasted_iota(jnp.int32, (c, LANES), 1)
    lo = lane < HEAD_DIM
    r = lax.broadcasted_iota(jnp.int32, (LANES, LANES), 0)
    cc = lax.broadcasted_iota(jnp.int32, (LANES, LANES), 1)
    same_head = (r < HEAD_DIM) == (cc < HEAD_DIM)
    mean_w = jnp.where(same_head, 1.0 / HEAD_DIM, 0.0).astype(BF16)
    for ci in range(n_chunks):
        rows = slice(ci * c, (ci + 1) * c)
        outs = []
        for g in range(GROUP_WIDTH // LANES):
            ls = slice(g * LANES, (g + 1) * LANES)
            qg = q_ref[rows, ls]
            kg = k_ref[rows, ls]
            vg = v_ref[rows, ls]
            zero = jnp.zeros_like(qg)
            s0 = _dot_nt(jnp.where(lo, qg, zero), kg) * dec_ref[2 * g]
            s1 = _dot_nt(jnp.where(lo, zero, qg), kg) * dec_ref[2 * g + 1]
            o_inner = jnp.where(lo, _dot(s0.astype(BF16), vg), _dot(s1.astype(BF16), vg))
            st = state[g]
            o = o_inner + _dot(qg, st.astype(BF16)) * cross_ref[:, ls]
            kd = kg.astype(F32) * kdec_ref[:, ls]
            upd = _dot(kd.T.astype(BF16), vg)
            state[g] = gam_ref[g] * st + jnp.where(same_head, upd, 0.0)
            o2 = o * o
            hi = o2.astype(BF16)
            ms = _dot(hi, mean_w) + _dot((o2 - hi.astype(F32)).astype(BF16), mean_w)
            of = o * lax.rsqrt(ms + EPS)
            outs.append(((of * gret_ref[:, ls]) * _silu(gate_ref[rows, ls])).astype(BF16))
        o_ref[rows, :] = jnp.concatenate(outs, axis=1)

    @pl.when(j == pl.num_programs(1) - 1)
    def _():
        st_ref[...] = state[...]


def _retention(q, k, v, gate, gret, st0, tables, *, n_chunks):
    b, s, w = q.shape
    tr = n_chunks * RET_CHUNK
    groups = w // LANES
    dec, cross, kdec, gam = tables
    tok = pl.BlockSpec((None, tr, w), lambda i, j: (i, j, 0))
    stspec = pl.BlockSpec((None, groups, LANES, LANES), lambda i, j: (i, 0, 0, 0))
    return pl.pallas_call(
        functools.partial(_ret_kernel, n_chunks=n_chunks),
        grid=(b, s // tr),
        in_specs=[tok, tok, tok, tok, _resident((1, w)), stspec, _resident(dec.shape), _resident(cross.shape),
                  _resident(kdec.shape), _resident(gam.shape)],
        out_specs=[tok, stspec],
        out_shape=[jax.ShapeDtypeStruct((b, s, w), BF16), jax.ShapeDtypeStruct((b, groups, LANES, LANES), F32)],
        scratch_shapes=[pltpu.VMEM((groups, LANES, LANES), F32)],
        compiler_params=_params("parallel", "arbitrary"),
        name="retention",
    )(q, k, v, gate, gret, st0, dec, cross, kdec, gam)


def _ret_tables(chunk_len):
    c = RET_CHUNK
    log_gamma = jnp.log(1.0 - 2.0 ** (-5.0 - jnp.arange(N_HEADS, dtype=F32)))
    idx = jnp.arange(c, dtype=F32)
    diff = idx[:, None] - idx[None, :]
    dec = jnp.where(diff >= 0, jnp.exp(jnp.maximum(diff, 0.0)[None] * log_gamma[:, None, None]), 0.0)
    cross = jnp.exp((idx + 1.0)[:, None] * log_gamma[None, :])
    kdec = jnp.exp((chunk_len - 1.0 - idx)[:, None] * log_gamma[None, :])
    sdec = jnp.exp(chunk_len * log_gamma)
    head_of_lane = jnp.arange(GROUP_WIDTH) // HEAD_DIM
    row_head = head_of_lane.reshape(GROUP_WIDTH // LANES, LANES)
    same = (jnp.arange(LANES)[:, None] // HEAD_DIM) == (jnp.arange(LANES)[None, :] // HEAD_DIM)
    gam = jnp.where(same[None], sdec[row_head][:, :, None], 0.0)
    return dec, cross[:, head_of_lane], kdec[:, head_of_lane], gam


def _blockdiag_states(st):
    b = st.shape[0]
    st = st.reshape(b, N_HEADS // 2, 2, HEAD_DIM, HEAD_DIM)
    z = jnp.zeros_like(st[:, :, 0])
    top = jnp.concatenate([st[:, :, 0], z], axis=-1)
    bot = jnp.concatenate([z, st[:, :, 1]], axis=-1)
    return jnp.concatenate([top, bot], axis=-2)


def _head_states(bd):
    b = bd.shape[0]
    a = bd[:, :, :HEAD_DIM, :HEAD_DIM]
    d = bd[:, :, HEAD_DIM:, HEAD_DIM:]
    return jnp.stack([a, d], axis=2).reshape(b, N_HEADS, HEAD_DIM, HEAD_DIM)


def _lane_roll(x, shift):
    return pltpu.roll(x, shift % x.shape[-1], x.ndim - 1)


def _logf_scan_kernel(lf_ref, suffix_ref, total_ref):
    y = lf_ref[...]
    n = y.shape[-1]
    lane = lax.broadcasted_iota(jnp.int32, y.shape, 1)
    shift = N_HEADS
    while shift < n:
        y = y + jnp.where(lane + shift < n, _lane_roll(y, -shift), 0.0)
        shift *= 2
    suffix_ref[...] = jnp.where(lane + N_HEADS < n, _lane_roll(y, -N_HEADS), 0.0)
    z = jnp.where(lane < N_HEADS, y, 0.0)
    shift = N_HEADS
    while shift < n:
        z = z + _lane_roll(z, shift)
        shift *= 2
    total_ref[...] = z


def _logf_scan(lf_flat, *, rows):
    n_pages, n = lf_flat.shape
    spec = pl.BlockSpec((rows, n), lambda i: (i, 0))
    return pl.pallas_call(
        _logf_scan_kernel,
        grid=(n_pages // rows,),
        in_specs=[spec],
        out_specs=[spec, spec],
        out_shape=[jax.ShapeDtypeStruct((n_pages, n), F32)] * 2,
        compiler_params=_params("parallel"),
        name="logf_page_scan",
    )(lf_flat)


def _fox_sample_kernel(pt_ref, q_ref, kn_ref, vn_ref, lfn_ref, *rest, pp):
    k_refs = rest[:pp]
    v_refs = rest[pp:2 * pp]
    suf_refs = rest[2 * pp:3 * pp]
    tot_refs = rest[3 * pp:4 * pp]
    o_ref = rest[4 * pp]
    m_s, l_s, acc_s, carry_s, fq_s = rest[4 * pp + 1:]
    del pt_ref
    j = pl.program_id(1)
    n_new = q_ref.shape[0]
    rows = n_new * N_HEADS
    page_rows = k_refs[0].shape[0] * N_HEADS
    q = q_ref[...].reshape(rows, HEAD_DIM).astype(BF16)
    r = lax.broadcasted_iota(jnp.int32, (rows, page_rows), 0)
    c = lax.broadcasted_iota(jnp.int32, (rows, page_rows), 1)
    head_mask = jnp.where((r % N_HEADS) == (c % N_HEADS), 0.0, -jnp.inf)

    @pl.when(j == 0)
    def _():
        lfn = lfn_ref[...]
        cr = lax.broadcasted_iota(jnp.int32, (LANES, LANES), 0)
        cl = lax.broadcasted_iota(jnp.int32, (LANES, LANES), 1)
        incl = jnp.where((cr <= cl) & ((cr % N_HEADS) == (cl % N_HEADS)), 1.0, 0.0).astype(BF16)
        cum = _dot_exact01(jnp.broadcast_to(lfn, (8, LANES)), incl)[0:1, :]
        rr = lax.broadcasted_iota(jnp.int32, (rows, LANES), 0)
        cn = lax.broadcasted_iota(jnp.int32, (rows, LANES), 1)
        fq = jnp.sum(jnp.where(rr == cn, cum, 0.0), axis=1, keepdims=True)
        fq_s[...] = fq
        kn = kn_ref[...].reshape(rows, HEAD_DIM).astype(BF16)
        vn = vn_ref[...].reshape(rows, HEAD_DIM).astype(BF16)
        s = _dot_nt(q, kn)
        rn = lax.broadcasted_iota(jnp.int32, (rows, rows), 0)
        cnn = lax.broadcasted_iota(jnp.int32, (rows, rows), 1)
        valid = ((rn % N_HEADS) == (cnn % N_HEADS)) & (cnn <= rn)
        logits = jnp.where(valid, s + fq - cum[:, :rows], -jnp.inf)
        m0 = jnp.max(logits, axis=1, keepdims=True)
        p = jnp.exp(logits - m0)
        m_s[...] = m0
        l_s[...] = jnp.sum(p, axis=1, keepdims=True)
        acc_s[...] = _dot(p.astype(BF16), vn)
        carry_s[...] = jnp.zeros_like(carry_s)

    fq = fq_s[...]
    carry = carry_s[...]
    ts = []
    for i in range(pp):
        kb = k_refs[i][...].reshape(page_rows, HEAD_DIM).astype(BF16)
        ts.append(_dot_nt(q, kb) + head_mask + (suf_refs[i][...] + carry))
        carry = carry + tot_refs[i][...]
    carry_s[...] = carry
    m_old = m_s[...]
    m_loc = ts[0].max(axis=1, keepdims=True)
    for t in ts[1:]:
        m_loc = jnp.maximum(m_loc, t.max(axis=1, keepdims=True))
    m_new = jnp.maximum(m_old, m_loc + fq)
    shift = fq - m_new
    alpha = jnp.exp(m_old - m_new)
    l_new = alpha * l_s[...]
    acc = alpha * acc_s[...]
    for i in range(pp):
        p = jnp.exp(ts[i] + shift)
        l_new = l_new + jnp.sum(p, axis=1, keepdims=True)
        vb = v_refs[i][...].reshape(page_rows, HEAD_DIM).astype(BF16)
        acc = acc + _dot(p.astype(BF16), vb)
    m_s[...] = m_new
    l_s[...] = l_new
    acc_s[...] = acc

    @pl.when(j == pl.num_programs(1) - 1)
    def _():
        o_ref[...] = (acc / l_new).reshape(o_ref.shape)


def _fox_sample(page_table, q, k_new, v_new, lf_new_flat, cache_k, cache_v, suffix, total, *, pp):
    b, n_new = q.shape[:2]
    n_pages = page_table.shape[1]
    page = cache_k.shape[1]
    rows = n_new * N_HEADS
    tok = pl.BlockSpec((None, n_new, N_HEADS, HEAD_DIM), lambda i, j, pt: (i, 0, 0, 0))

    def paged(shape, slot):
        nd = len(shape)
        return pl.BlockSpec((None,) + shape,
                            lambda i, j, pt: (pt[i, n_pages - 1 - (j * pp + slot)],) + (0,) * nd)

    kv = [paged((page, N_HEADS, HEAD_DIM), s) for s in range(pp)]
    vec = [paged((1, page * N_HEADS), s) for s in range(pp)]
    grid_spec = pltpu.PrefetchScalarGridSpec(
        num_scalar_prefetch=1,
        grid=(b, n_pages // pp),
        in_specs=[tok, tok, tok, pl.BlockSpec((None, 1, LANES), lambda i, j, pt: (i, 0, 0))] + kv + kv + vec + vec,
        out_specs=tok,
        scratch_shapes=[pltpu.VMEM((rows, 1), F32), pltpu.VMEM((rows, 1), F32), pltpu.VMEM((rows, HEAD_DIM), F32),
                        pltpu.VMEM((1, page * N_HEADS), F32), pltpu.VMEM((rows, 1), F32)],
    )
    return pl.pallas_call(
        functools.partial(_fox_sample_kernel, pp=pp),
        grid_spec=grid_spec,
        out_shape=jax.ShapeDtypeStruct(q.shape, F32),
        compiler_params=_params("parallel", "arbitrary"),
        name="fox_sample_paged",
    )(page_table, q, k_new, v_new, lf_new_flat, *([cache_k] * pp), *([cache_v] * pp),
      *([suffix] * pp), *([total] * pp))


def _outproj_kernel(x_ref, of_ref, or_ref, wo_ref, g_ref, wq_ref, x2_ref, q_ref):
    w = GROUP_WIDTH
    x2 = x_ref[...] + _dot(of_ref[...], wo_ref[0:w, :]) + _dot(or_ref[...], wo_ref[w:2 * w, :])
    x2_ref[...] = x2
    q_ref[...] = _dot(_rms(x2, g_ref[...]).astype(BF16), wq_ref[...]) * Q_SCALE


def _outproj(x, o_fox, o_ret, w_out, g_cross, w_cq, *, tm):
    t, d = x.shape
    w = GROUP_WIDTH
    mw = w_cq.shape[1]
    row = lambda width: pl.BlockSpec((tm, width), lambda i: (i, 0))
    return pl.pallas_call(
        _outproj_kernel,
        grid=(t // tm,),
        in_specs=[row(d), row(w), row(w), _resident(w_out.shape), _resident((1, d)), _resident(w_cq.shape)],
        out_specs=[row(d), row(mw)],
        out_shape=[jax.ShapeDtypeStruct((t, d), F32), jax.ShapeDtypeStruct((t, mw), F32)],
        compiler_params=_params("parallel"),
        name="outproj",
    )(x, o_fox, o_ret, w_out, g_cross, w_cq)


def _xattn_kernel(x_ref, q_ref, mk_ref, mv_ref, wo_ref, o_ref, *, group):
    tt, mw = q_ref.shape
    n_keys = mk_ref.shape[0]
    q = q_ref[...]
    lane = lax.broadcasted_iota(jnp.int32, (tt, mw), 1)
    qs = jnp.concatenate([jnp.where(lane // HEAD_DIM == h, q, 0.0) for h in range(N_MEM_HEADS)], axis=0)
    s = _dot_nt(qs.astype(BF16), mk_ref[...].astype(BF16))
    if group is not None:
        tok_b = (lax.broadcasted_iota(jnp.int32, s.shape, 0) % tt) // group[0]
        key_b = lax.broadcasted_iota(jnp.int32, s.shape, 1) // group[1]
        s = jnp.where(tok_b == key_b, s, -jnp.inf)
    e = jnp.exp(s - jnp.max(s, axis=1, keepdims=True))
    p = (e / jnp.sum(e, axis=1, keepdims=True)).astype(BF16)
    pv = _dot(p, mv_ref[...].astype(BF16))
    o = jnp.zeros((tt, mw), F32)
    for h in range(N_MEM_HEADS):
        o = o + jnp.where(lane // HEAD_DIM == h, pv[h * tt:(h + 1) * tt], 0.0)
    o_ref[...] = x_ref[...] + _dot(o.astype(BF16), wo_ref[...])
    del n_keys


def _xattn(x, q, mk, mv, w_co, *, tt, keys_per_block, mem_index, group):
    t, d = x.shape
    mw = q.shape[1]
    row = lambda width: pl.BlockSpec((tt, width), lambda i: (i, 0))
    mem = pl.BlockSpec((keys_per_block, mw), lambda i: (mem_index(i), 0))
    return pl.pallas_call(
        functools.partial(_xattn_kernel, group=group),
        grid=(t // tt,),
        in_specs=[row(d), row(mw), mem, mem, _resident(w_co.shape)],
        out_specs=row(d),
        out_shape=jax.ShapeDtypeStruct((t, d), F32),
        compiler_params=_params("parallel"),
        name="cross_attention",
    )(x, q, mk, mv, w_co)


def _memkv_kernel(m_ref, g_ref, wk_ref, wv_ref, k_ref, v_ref):
    h = _rms(m_ref[...], g_ref[...]).astype(BF16)
    k_ref[...] = _dot(h, wk_ref[...])
    v_ref[...] = _dot(h, wv_ref[...])


def _memkv(mem, g, wk, wv, *, tm):
    t, d = mem.shape
    mw = wk.shape[1]
    row = lambda width: pl.BlockSpec((tm, width), lambda i: (i, 0))
    return pl.pallas_call(
        _memkv_kernel,
        grid=(t // tm,),
        in_specs=[row(d), _resident((1, d)), _resident(wk.shape), _resident(wv.shape)],
        out_specs=[row(mw), row(mw)],
        out_shape=[jax.ShapeDtypeStruct((t, mw), F32)] * 2,
        compiler_params=_params("parallel"),
        name="memory_kv",
    )(mem, g, wk, wv)


def _rope_tables(pos):
    half = HEAD_DIM // 2
    inv = ROPE_THETA ** (-jnp.arange(half, dtype=F32) / half)
    ang = pos.astype(F32)[:, None] * inv[None, :]
    cos = jnp.cos(ang)
    sin = jnp.sin(ang)
    return (jnp.tile(jnp.concatenate([cos, cos], axis=-1), (1, N_HEADS)),
            jnp.tile(jnp.concatenate([-sin, sin], axis=-1), (1, N_HEADS)))


def _token_tile(t):
    return 512 if t % 512 == 0 else t


def kernel(x_prompt, x_sample, mem_prompt, cache_fox_k, cache_fox_v, cache_fox_logf, state_ret, cache_mem_k, cache_mem_v, page_table, g_ffn1, w1_gate, w1_up, w1_down, g_mix, w_in, b_f, g_ret, w_out, g_cross, g_mem, w_cq, w_ck, w_cv, w_co, g_ffn2, w2_gate, w2_up, w2_down, g_final):
    depth = w_in.shape[0]
    assert depth == 1, "single-layer trunk"
    b, s, d = x_prompt.shape
    db, ds, _ = x_sample.shape
    n_mem = mem_prompt.shape[1]
    page = cache_fox_k.shape[2]
    past_len = page_table.shape[1] * page
    w = GROUP_WIDTH
    assert s % RET_CHUNK == 0 and ds <= RET_CHUNK

    row = lambda a: a.reshape(1, -1)
    bf = lambda a: a.astype(BF16)
    wi = w_in[0]
    fox_end = 3 * w
    wfox = bf(wi[:, :fox_end])
    wf = bf(jnp.pad(wi[:, fox_end:fox_end + N_HEADS], ((0, 0), (0, LANES - N_HEADS))))
    bfp = jnp.pad(row(b_f[0]), ((0, 0), (0, LANES - N_HEADS)))
    ret0 = fox_end + N_HEADS
    wq_r, wk_r, wv_r, wg_r = (wi[:, ret0 + i * w:ret0 + (i + 1) * w] for i in range(4))
    lane = jnp.arange(w)
    swap = jnp.where(lane % HEAD_DIM < HEAD_DIM // 2, lane + HEAD_DIM // 2, lane - HEAD_DIM // 2)
    wret = bf(jnp.concatenate([wq_r, wq_r[:, swap], wk_r, wk_r[:, swap], wv_r, wg_r], axis=1))
    w1 = (bf(w1_gate[0]), bf(w1_up[0]), bf(w1_down[0]))
    w2 = (bf(w2_gate[0]), bf(w2_up[0]), bf(w2_down[0]))
    wo, wcq, wck, wcv, wco = bf(w_out[0]), bf(w_cq[0]), bf(w_ck[0]), bf(w_cv[0]), bf(w_co[0])
    gf = row(g_final)

    def trunk_in(x, pos):
        tm = _token_tile(x.shape[0])
        x1 = _ffn(x, row(g_ffn1[0]), *w1, gf, final_norm=False, tm=tm)
        cos, sin = _rope_tables(pos)
        return (x1,) + tuple(_inproj(x1, row(g_mix[0]), wfox, wf, bfp, wret, cos, sin, tm=tm))

    def trunk_out(x1, o_fox, o_ret, mk, mv, *, tt, keys_per_block, mem_index, group):
        tm = _token_tile(x1.shape[0])
        x2, qc = _outproj(x1, o_fox, o_ret, wo, row(g_cross[0]), wcq, tm=tm)
        x3 = _xattn(x2, qc, mk, mv, wco, tt=tt, keys_per_block=keys_per_block, mem_index=mem_index, group=group)
        return _ffn(x3, row(g_ffn2[0]), *w2, gf, final_norm=True, tm=tm)

    xp1, qa, ka, va, kab, vab, lf, qr, kr, vr, gr = trunk_in(x_prompt.reshape(b * s, d), jnp.arange(s, dtype=jnp.int32))
    tq = 256
    fcol, frow = _cumsum(jnp.pad(lf.reshape(b, s, N_HEADS), ((0, 0), (0, 0), (0, LANES - N_HEADS))), tc=512)
    o_fox = _fox_prompt(bf(qa).reshape(b, s, w), kab.reshape(b, s, w), vab.reshape(b, s, w), fcol,
                        frow.reshape(b, N_HEADS, s // tq, tq), tq=tq)
    st_zero = jnp.zeros((b, w // LANES, LANES, LANES), F32)
    o_ret, st_p = _retention(qr.reshape(b, s, w), kr.reshape(b, s, w), vr.reshape(b, s, w), gr.reshape(b, s, w),
                             row(g_ret[0]), st_zero, _ret_tables(RET_CHUNK), n_chunks=4)
    mk, mv = _memkv(mem_prompt.reshape(b * n_mem, d), row(g_mem[0]), wck, wcv, tm=_token_tile(b * n_mem))
    tt = 512
    y_prompt = trunk_out(xp1, o_fox.reshape(b * s, w), o_ret.reshape(b * s, w), mk, mv, tt=tt,
                         keys_per_block=n_mem, mem_index=lambda i: i // (s // tt), group=None)

    pos_s = past_len + jnp.arange(ds, dtype=jnp.int32)
    xs1, qa_s, ka_s, va_s, _, _, lf_s, qr_s, kr_s, vr_s, gr_s = trunk_in(x_sample.reshape(db * ds, d), jnp.tile(pos_s, db))
    heads = lambda a: a.reshape(db, ds, N_HEADS, HEAD_DIM)
    lf_flat = cache_fox_logf[0].reshape(-1, page * N_HEADS)
    suffix, total = _logf_scan(lf_flat, rows=512 if lf_flat.shape[0] % 512 == 0 else lf_flat.shape[0])
    lf_new = jnp.pad(lf_s.reshape(db, 1, ds * N_HEADS), ((0, 0), (0, 0), (0, LANES - ds * N_HEADS)))
    o_fox_s = _fox_sample(page_table, heads(qa_s), heads(ka_s), heads(va_s), lf_new, cache_fox_k[0], cache_fox_v[0],
                          suffix.reshape(-1, 1, page * N_HEADS), total.reshape(-1, 1, page * N_HEADS), pp=8)
    pad_tok = lambda a: jnp.pad(a.reshape(db, ds, w), ((0, 0), (0, RET_CHUNK - ds), (0, 0)))
    o_ret_s, st_s = _retention(pad_tok(qr_s), pad_tok(kr_s), pad_tok(vr_s), pad_tok(gr_s), row(g_ret[0]),
                               _blockdiag_states(state_ret[0]), _ret_tables(ds), n_chunks=1)
    bb = 8
    y_sample = trunk_out(xs1, bf(o_fox_s.reshape(db * ds, w)), o_ret_s[:, :ds].reshape(db * ds, w),
                         cache_mem_k[0].reshape(db * n_mem, -1), cache_mem_v[0].reshape(db * n_mem, -1),
                         tt=bb * ds, keys_per_block=bb * n_mem, mem_index=lambda i: i, group=(ds, n_mem))

    mem_heads = lambda a: a.reshape(1, b, n_mem, N_MEM_HEADS, -1)
    return (y_prompt.reshape(b, s, d), y_sample.reshape(db, ds, d),
            ka.reshape(1, b, s, N_HEADS, HEAD_DIM), va.reshape(1, b, s, N_HEADS, HEAD_DIM),
            lf.reshape(1, b, s, N_HEADS), _head_states(st_p)[None], mem_heads(mk), mem_heads(mv),
            ka_s.reshape(1, db, ds, N_HEADS, HEAD_DIM), va_s.reshape(1, db, ds, N_HEADS, HEAD_DIM),
            lf_s.reshape(1, db, ds, N_HEADS), _head_states(st_s)[None])
```

```python
import functools

import jax
import jax.numpy as jnp
from jax import lax
from jax.experimental import pallas as pl
from jax.experimental.pallas import tpu as pltpu

F32 = jnp.float32
BF16 = jnp.bfloat16

HEAD_DIM = 64
N_HEADS = 8
GROUP_WIDTH = N_HEADS * HEAD_DIM
N_MEM_HEADS = 4
ROPE_THETA = 10000.0
EPS = 1e-6
FFN_RES = 0.5
Q_SCALE = HEAD_DIM ** -0.5
RET_CHUNK = 128
LANES = 128
VMEM_LIMIT = 56 * 1024 * 1024


def _params(*sem):
    return pltpu.CompilerParams(dimension_semantics=sem, vmem_limit_bytes=VMEM_LIMIT)


def _resident(shape):
    nd = len(shape)
    return pl.BlockSpec(shape, lambda *_: (0,) * nd, pipeline_mode=pl.Buffered(1))


def _dot(a, b):
    return jnp.dot(a, b, preferred_element_type=F32)


def _dot_nt(a, b):
    return lax.dot_general(a, b, (((1,), (1,)), ((), ())), preferred_element_type=F32)


def _rms(x, g):
    return x * lax.rsqrt(jnp.mean(x * x, axis=-1, keepdims=True) + EPS) * g


def _silu(x):
    return x * jax.nn.sigmoid(x)


def _split3(x):
    hi = x.astype(BF16)
    r1 = x - hi.astype(F32)
    mid = r1.astype(BF16)
    lo = (r1 - mid.astype(F32)).astype(BF16)
    return hi, mid, lo


def _dot_exact01(x, w01, w_left=False):
    mm = (lambda part: _dot(w01, part)) if w_left else (lambda part: _dot(part, w01))
    hi, mid, lo = _split3(x)
    return mm(hi) + mm(mid) + mm(lo)


def _ffn_kernel(x_ref, g_ref, wg_ref, wu_ref, wd_ref, gf_ref, o_ref, *, chunks, final_norm):
    x = x_ref[...]
    h = _rms(x, g_ref[...]).astype(BF16)
    acc = jnp.zeros_like(x)
    start = 0
    for width in chunks:
        sl = slice(start, start + width)
        a = _silu(_dot(h, wg_ref[:, sl])) * _dot(h, wu_ref[:, sl])
        acc = acc + _dot(a.astype(BF16), wd_ref[sl, :])
        start += width
    y = x + FFN_RES * acc
    if final_norm:
        y = _rms(y, gf_ref[...])
    o_ref[...] = y


def _ffn(x, g, wg, wu, wd, gf, *, final_norm, tm):
    t, d = x.shape
    d_ff = wg.shape[1]
    chunks = [512] * (d_ff // 512)
    if d_ff % 512:
        chunks.append(d_ff % 512)
    row = pl.BlockSpec((tm, d), lambda i: (i, 0))
    return pl.pallas_call(
        functools.partial(_ffn_kernel, chunks=tuple(chunks), final_norm=final_norm),
        grid=(t // tm,),
        in_specs=[row, _resident((1, d)), _resident(wg.shape), _resident(wu.shape),
                  _resident(wd.shape), _resident((1, d))],
        out_specs=row,
        out_shape=jax.ShapeDtypeStruct((t, d), F32),
        compiler_params=_params("parallel"),
        name="ffn",
    )(x, g, wg, wu, wd, gf)


def _inproj_kernel(x_ref, g_ref, wfox_ref, wf_ref, bf_ref, wret_ref, cos_ref, sin_ref,
                   qa_ref, ka_ref, va_ref, kab_ref, vab_ref, lf_ref, qr_ref, kr_ref, vr_ref, gr_ref):
    h = _rms(x_ref[...], g_ref[...]).astype(BF16)
    w = GROUP_WIDTH
    qa_ref[...] = (_dot(h, wfox_ref[:, 0:w]) * Q_SCALE).astype(BF16)
    ka = _dot(h, wfox_ref[:, w:2 * w])
    ka_ref[...] = ka
    kab_ref[...] = ka.astype(BF16)
    va = _dot(h, wfox_ref[:, 2 * w:3 * w])
    va_ref[...] = va
    vab_ref[...] = va.astype(BF16)
    z = _dot(h, wf_ref[...]) + bf_ref[...]
    lf_ref[...] = (jnp.minimum(z, 0.0) - jnp.log1p(jnp.exp(-jnp.abs(z))))[:, :N_HEADS]
    cos = cos_ref[...]
    sin = sin_ref[...]
    qr_ref[...] = (_dot(h, wret_ref[:, 0:w]) * cos + _dot(h, wret_ref[:, w:2 * w]) * sin).astype(BF16)
    kr = _dot(h, wret_ref[:, 2 * w:3 * w]) * cos + _dot(h, wret_ref[:, 3 * w:4 * w]) * sin
    kr_ref[...] = (kr * Q_SCALE).astype(BF16)
    vr_ref[...] = _dot(h, wret_ref[:, 4 * w:5 * w]).astype(BF16)
    gr_ref[...] = _dot(h, wret_ref[:, 5 * w:6 * w])


def _inproj(x, g, wfox, wf, bf, wret, cos, sin, *, tm):
    t, d = x.shape
    w = GROUP_WIDTH
    n_pos = cos.shape[0] // tm
    row = lambda width: pl.BlockSpec((tm, width), lambda i: (i, 0))
    tab = pl.BlockSpec((tm, w), lambda i: (i % n_pos, 0))
    sds = lambda width, dt: jax.ShapeDtypeStruct((t, width), dt)
    return pl.pallas_call(
        _inproj_kernel,
        grid=(t // tm,),
        in_specs=[row(d), _resident((1, d)), _resident(wfox.shape), _resident(wf.shape),
                  _resident(bf.shape), _resident(wret.shape), tab, tab],
        out_specs=[row(w), row(w), row(w), row(w), row(w), row(N_HEADS), row(w), row(w), row(w), row(w)],
        out_shape=[sds(w, BF16), sds(w, F32), sds(w, F32), sds(w, BF16), sds(w, BF16), sds(N_HEADS, F32),
                   sds(w, BF16), sds(w, BF16), sds(w, BF16), sds(w, F32)],
        compiler_params=_params("parallel"),
        name="inproj",
    )(x, g, wfox, wf, bf, wret, cos, sin)


def _cumsum_kernel(lf_ref, fcol_ref, frow_ref, carry_col, carry_row, *, tc):
    @pl.when(pl.program_id(1) == 0)
    def _():
        carry_col[...] = jnp.zeros_like(carry_col)
        carry_row[...] = jnp.zeros_like(carry_row)

    lf = lf_ref[...]
    r = lax.broadcasted_iota(jnp.int32, (tc, tc), 0)
    c = lax.broadcasted_iota(jnp.int32, (tc, tc), 1)
    lower = jnp.where(c <= r, 1.0, 0.0).astype(BF16)
    fcol = _dot_exact01(lf, lower, w_left=True) + carry_col[...]
    fcol_ref[...] = fcol
    carry_col[...] = fcol[tc - 1:tc, :]
    upper = jnp.where(r <= c, 1.0, 0.0).astype(BF16)
    frow = _dot_exact01(lf.T[:N_HEADS, :], upper) + carry_row[...]
    frow_ref[...] = frow
    carry_row[...] = frow[:, tc - 1:tc]


def _cumsum(lf_pad, *, tc):
    b, s, _ = lf_pad.shape
    return pl.pallas_call(
        functools.partial(_cumsum_kernel, tc=tc),
        grid=(b, s // tc),
        in_specs=[pl.BlockSpec((None, tc, LANES), lambda i, j: (i, j, 0))],
        out_specs=[pl.BlockSpec((None, tc, LANES), lambda i, j: (i, j, 0)),
                   pl.BlockSpec((None, N_HEADS, tc), lambda i, j: (i, 0, j))],
        out_shape=[jax.ShapeDtypeStruct((b, s, LANES), F32), jax.ShapeDtypeStruct((b, N_HEADS, s), F32)],
        scratch_shapes=[pltpu.VMEM((1, LANES), F32), pltpu.VMEM((N_HEADS, 1), F32)],
        compiler_params=_params("parallel", "arbitrary"),
        name="logf_cumsum",
    )(lf_pad)


def _fox_prompt_kernel(q_ref, k_ref, v_ref, fcol_ref, frow_ref, o_ref, *, tq):
    hp = pl.program_id(1)
    qi = pl.program_id(2)
    lane = lax.broadcasted_iota(jnp.int32, (tq, LANES), 1)
    lo = lane < HEAD_DIM
    q = q_ref[...]
    zero = jnp.zeros_like(q)
    q_h = (jnp.where(lo, q, zero), jnp.where(lo, zero, q))
    fcol = fcol_ref[...]
    fq = tuple(jnp.sum(jnp.where(lane == 2 * hp + h, fcol, 0.0), axis=1, keepdims=True) for h in (0, 1))

    def block(kj, carry, masked):
        start = pl.multiple_of(kj * tq, tq)
        kb = k_ref[pl.ds(start, tq), :]
        vb = v_ref[pl.ds(start, tq), :]
        new = []
        for h in (0, 1):
            m_old, l_old, acc = carry[h]
            fk = frow_ref[2 * hp + h, pl.ds(kj, 1), :]
            t = _dot_nt(q_h[h], kb) - fk
            if masked:
                r = lax.broadcasted_iota(jnp.int32, (tq, tq), 0)
                c = lax.broadcasted_iota(jnp.int32, (tq, tq), 1)
                t = jnp.where(c <= r, t, -jnp.inf)
            m_new = jnp.maximum(m_old, jnp.max(t, axis=1, keepdims=True) + fq[h])
            p = jnp.exp(t + (fq[h] - m_new))
            alpha = jnp.exp(m_old - m_new)
            new.append((m_new, alpha * l_old + jnp.sum(p, axis=1, keepdims=True),
                        alpha * acc + _dot(p.astype(BF16), vb)))
        return tuple(new)

    init = tuple((jnp.full((tq, 1), -1e30, F32), jnp.zeros((tq, 1), F32), jnp.zeros((tq, LANES), F32))
                 for _ in (0, 1))
    carry = lax.fori_loop(0, qi, lambda kj, c: block(kj, c, False), init)
    (_, l0, a0), (_, l1, a1) = block(qi, carry, True)
    o_ref[...] = jnp.where(lo, a0 / l0, a1 / l1).astype(o_ref.dtype)


def _fox_prompt(q, k, v, fcol, frow4, *, tq):
    b, s, w = q.shape
    pairs = w // LANES
    qspec = pl.BlockSpec((None, tq, LANES), lambda i, p, j: (i, j, p))
    kvspec = pl.BlockSpec((None, s, LANES), lambda i, p, j: (i, 0, p))
    return pl.pallas_call(
        functools.partial(_fox_prompt_kernel, tq=tq),
        grid=(b, pairs, s // tq),
        in_specs=[qspec, kvspec, kvspec,
                  pl.BlockSpec((None, tq, LANES), lambda i, p, j: (i, j, 0)),
                  pl.BlockSpec((None, N_HEADS, s // tq, tq), lambda i, p, j: (i, 0, 0, 0))],
        out_specs=qspec,
        out_shape=jax.ShapeDtypeStruct((b, s, w), BF16),
        compiler_params=_params("parallel", "parallel", "arbitrary"),
        name="fox_prompt",
    )(q, k, v, fcol, frow4)


def _ret_kernel(q_ref, k_ref, v_ref, gate_ref, gret_ref, st0_ref, dec_ref, cross_ref, kdec_ref, gam_ref,
                o_ref, st_ref, state, *, n_chunks):
    c = RET_CHUNK
    j = pl.program_id(1)

    @pl.when(j == 0)
    def _():
        state[...] = st0_ref[...]

    lane = lax.broadcasted_iota(jnp.int32, (c, LANES), 1)
    lo = lane < HEAD_DIM
    r = lax.broadcasted_iota(jnp.int32, (LANES, LANES), 0)
    cc = lax.broadcasted_iota(jnp.int32, (LANES, LANES), 1)
    same_head = (r < HEAD_DIM) == (cc < HEAD_DIM)
    mean_w = jnp.where(same_head, 1.0 / HEAD_DIM, 0.0).astype(BF16)
    for ci in range(n_chunks):
        rows = slice(ci * c, (ci + 1) * c)
        outs = []
        for g in range(GROUP_WIDTH // LANES):
            ls = slice(g * LANES, (g + 1) * LANES)
            qg = q_ref[rows, ls]
            kg = k_ref[rows, ls]
            vg = v_ref[rows, ls]
            zero = jnp.zeros_like(qg)
            s0 = _dot_nt(jnp.where(lo, qg, zero), kg) * dec_ref[2 * g]
            s1 = _dot_nt(jnp.where(lo, zero, qg), kg) * dec_ref[2 * g + 1]
            o_inner = jnp.where(lo, _dot(s0.astype(BF16), vg), _dot(s1.astype(BF16), vg))
            st = state[g]
            o = o_inner + _dot(qg, st.astype(BF16)) * cross_ref[:, ls]
            kd = kg.astype(F32) * kdec_ref[:, ls]
            upd = _dot(kd.T.astype(BF16), vg)
            state[g] = gam_ref[g] * st + jnp.where(same_head, upd, 0.0)
            o2 = o * o
            hi = o2.astype(BF16)
            ms = _dot(hi, mean_w) + _dot((o2 - hi.astype(F32)).astype(BF16), mean_w)
            of = o * lax.rsqrt(ms + EPS)
            outs.append(((of * gret_ref[:, ls]) * _silu(gate_ref[rows, ls])).astype(BF16))
        o_ref[rows, :] = jnp.concatenate(outs, axis=1)

    @pl.when(j == pl.num_programs(1) - 1)
    def _():
        st_ref[...] = state[...]


def _retention(q, k, v, gate, gret, st0, tables, *, n_chunks):
    b, s, w = q.shape
    tr = n_chunks * RET_CHUNK
    groups = w // LANES
    dec, cross, kdec, gam = tables
    tok = pl.BlockSpec((None, tr, w), lambda i, j: (i, j, 0))
    stspec = pl.BlockSpec((None, groups, LANES, LANES), lambda i, j: (i, 0, 0, 0))
    return pl.pallas_call(
        functools.partial(_ret_kernel, n_chunks=n_chunks),
        grid=(b, s // tr),
        in_specs=[tok, tok, tok, tok, _resident((1, w)), stspec, _resident(dec.shape), _resident(cross.shape),
                  _resident(kdec.shape), _resident(gam.shape)],
        out_specs=[tok, stspec],
        out_shape=[jax.ShapeDtypeStruct((b, s, w), BF16), jax.ShapeDtypeStruct((b, groups, LANES, LANES), F32)],
        scratch_shapes=[pltpu.VMEM((groups, LANES, LANES), F32)],
        compiler_params=_params("parallel", "arbitrary"),
        name="retention",
    )(q, k, v, gate, gret, st0, dec, cross, kdec, gam)


def _ret_tables(chunk_len):
    c = RET_CHUNK
    log_gamma = jnp.log(1.0 - 2.0 ** (-5.0 - jnp.arange(N_HEADS, dtype=F32)))
    idx = jnp.arange(c, dtype=F32)
    diff = idx[:, None] - idx[None, :]
    dec = jnp.where(diff >= 0, jnp.exp(jnp.maximum(diff, 0.0)[None] * log_gamma[:, None, None]), 0.0)
    cross = jnp.exp((idx + 1.0)[:, None] * log_gamma[None, :])
    kdec = jnp.exp((chunk_len - 1.0 - idx)[:, None] * log_gamma[None, :])
    sdec = jnp.exp(chunk_len * log_gamma)
    head_of_lane = jnp.arange(GROUP_WIDTH) // HEAD_DIM
    row_head = head_of_lane.reshape(GROUP_WIDTH // LANES, LANES)
    same = (jnp.arange(LANES)[:, None] // HEAD_DIM) == (jnp.arange(LANES)[None, :] // HEAD_DIM)
    gam = jnp.where(same[None], sdec[row_head][:, :, None], 0.0)
    return dec, cross[:, head_of_lane], kdec[:, head_of_lane], gam


def _blockdiag_states(st):
    b = st.shape[0]
    st = st.reshape(b, N_HEADS // 2, 2, HEAD_DIM, HEAD_DIM)
    z = jnp.zeros_like(st[:, :, 0])
    top = jnp.concatenate([st[:, :, 0], z], axis=-1)
    bot = jnp.concatenate([z, st[:, :, 1]], axis=-1)
    return jnp.concatenate([top, bot], axis=-2)


def _head_states(bd):
    b = bd.shape[0]
    a = bd[:, :, :HEAD_DIM, :HEAD_DIM]
    d = bd[:, :, HEAD_DIM:, HEAD_DIM:]
    return jnp.stack([a, d], axis=2).reshape(b, N_HEADS, HEAD_DIM, HEAD_DIM)


def _fox_sample_kernel(pt_ref, q_ref, kn_ref, vn_ref, lfn_ref, *rest, pp):
    k_refs = rest[:pp]
    v_refs = rest[pp:2 * pp]
    lf_refs = rest[2 * pp:3 * pp]
    o_ref = rest[3 * pp]
    qbd_s, m_s, l_s, acc_s, carry_s, fq_s = rest[3 * pp + 1:]
    del pt_ref
    j = pl.program_id(1)
    n_new, w = q_ref.shape
    rows = n_new * N_HEADS
    page = k_refs[0].shape[1]
    sub = lax.broadcasted_iota(jnp.int32, (N_HEADS, w), 0)
    own = (lax.broadcasted_iota(jnp.int32, (N_HEADS, w), 1) // HEAD_DIM) == sub
    tr = lax.broadcasted_iota(jnp.int32, (page, page), 0)
    tc = lax.broadcasted_iota(jnp.int32, (page, page), 1)
    tile_rows = lambda x: jnp.concatenate([x] * n_new, axis=0)

    @pl.when(j == 0)
    def _():
        q = q_ref[...]
        qbd = jnp.concatenate([jnp.where(own, jnp.broadcast_to(q[t:t + 1, :], (N_HEADS, w)), 0.0)
                               for t in range(n_new)], axis=0).astype(BF16)
        qbd_s[...] = qbd
        incl = jnp.where(tr <= tc, 1.0, 0.0).astype(BF16)
        cum = _dot_exact01(lfn_ref[...], incl)
        fq = jnp.concatenate([cum[:, t:t + 1] for t in range(n_new)], axis=0)
        fq_s[...] = fq
        s = _dot(qbd, kn_ref[...].astype(BF16))
        t_of_row = lax.broadcasted_iota(jnp.int32, (rows, page), 0) // N_HEADS
        m_idx = lax.broadcasted_iota(jnp.int32, (rows, page), 1)
        logits = jnp.where(m_idx <= t_of_row, s + fq - tile_rows(cum), -jnp.inf)
        m0 = jnp.max(logits, axis=1, keepdims=True)
        p = jnp.exp(logits - m0)
        m_s[...] = m0
        l_s[...] = jnp.sum(p, axis=1, keepdims=True)
        acc_s[...] = _dot_nt(p.astype(BF16), vn_ref[...].astype(BF16))
        carry_s[...] = jnp.zeros_like(carry_s)

    lfs = [lf_refs[i][...] for i in range(pp)]
    strict = jnp.where(tr > tc, 1.0, 0.0).astype(BF16)
    suffix = _dot_exact01(jnp.concatenate(lfs, axis=0), strict)
    carry = carry_s[...]
    biases = []
    for i in range(pp):
        d_i = suffix[i * N_HEADS:(i + 1) * N_HEADS]
        biases.append(tile_rows(d_i + carry))
        carry = carry + d_i[:, 0:1] + lfs[i][:, 0:1]
    carry_s[...] = carry
    kcat = jnp.concatenate([k_refs[i][...].astype(BF16) for i in range(pp)], axis=1)
    t = _dot(qbd_s[...], kcat) + jnp.concatenate(biases, axis=1)
    fq = fq_s[...]
    m_old = m_s[...]
    m_new = jnp.maximum(m_old, jnp.max(t, axis=1, keepdims=True) + fq)
    p = jnp.exp(t + (fq - m_new))
    alpha = jnp.exp(m_old - m_new)
    vcat = jnp.concatenate([v_refs[i][...].astype(BF16) for i in range(pp)], axis=1)
    l_new = alpha * l_s[...] + jnp.sum(p, axis=1, keepdims=True)
    acc = alpha * acc_s[...] + _dot_nt(p.astype(BF16), vcat)
    m_s[...] = m_new
    l_s[...] = l_new
    acc_s[...] = acc

    @pl.when(j == pl.num_programs(1) - 1)
    def _():
        out = acc / l_new
        o_ref[...] = jnp.concatenate(
            [jnp.sum(jnp.where(own, out[t * N_HEADS:(t + 1) * N_HEADS], 0.0), axis=0, keepdims=True)
             for t in range(n_new)], axis=0)


def _fox_sample(page_table, q, k_new_t, v_new_t, lf_new_t, cache_k_t, cache_v_t, cache_lf_t, *, pp):
    b, n_new, w = q.shape
    n_pages = page_table.shape[1]
    page = cache_k_t.shape[2]
    rows = n_new * N_HEADS
    per_row = lambda shape: pl.BlockSpec((None,) + shape, lambda i, j, pt: (i, 0, 0))

    def paged(shape, slot):
        return pl.BlockSpec((None,) + shape, lambda i, j, pt: (pt[i, n_pages - 1 - (j * pp + slot)], 0, 0))

    kv = [paged((w, page), s) for s in range(pp)]
    lf = [paged((N_HEADS, page), s) for s in range(pp)]
    grid_spec = pltpu.PrefetchScalarGridSpec(
        num_scalar_prefetch=1,
        grid=(b, n_pages // pp),
        in_specs=[per_row((n_new, w)), per_row((w, page)), per_row((w, page)), per_row((N_HEADS, page))] + kv + kv + lf,
        out_specs=per_row((n_new, w)),
        scratch_shapes=[pltpu.VMEM((rows, w), BF16), pltpu.VMEM((rows, 1), F32), pltpu.VMEM((rows, 1), F32),
                        pltpu.VMEM((rows, w), F32), pltpu.VMEM((N_HEADS, 1), F32), pltpu.VMEM((rows, 1), F32)],
    )
    return pl.pallas_call(
        functools.partial(_fox_sample_kernel, pp=pp),
        grid_spec=grid_spec,
        out_shape=jax.ShapeDtypeStruct(q.shape, F32),
        compiler_params=_params("parallel", "arbitrary"),
        name="fox_sample_paged",
    )(page_table, q, k_new_t, v_new_t, lf_new_t, *([cache_k_t] * pp), *([cache_v_t] * pp), *([cache_lf_t] * pp))


def _outproj_kernel(x_ref, of_ref, or_ref, wo_ref, g_ref, wq_ref, x2_ref, q_ref):
    w = GROUP_WIDTH
    x2 = x_ref[...] + _dot(of_ref[...], wo_ref[0:w, :]) + _dot(or_ref[...], wo_ref[w:2 * w, :])
    x2_ref[...] = x2
    q_ref[...] = _dot(_rms(x2, g_ref[...]).astype(BF16), wq_ref[...]) * Q_SCALE


def _outproj(x, o_fox, o_ret, w_out, g_cross, w_cq, *, tm):
    t, d = x.shape
    w = GROUP_WIDTH
    mw = w_cq.shape[1]
    row = lambda width: pl.BlockSpec((tm, width), lambda i: (i, 0))
    return pl.pallas_call(
        _outproj_kernel,
        grid=(t // tm,),
        in_specs=[row(d), row(w), row(w), _resident(w_out.shape), _resident((1, d)), _resident(w_cq.shape)],
        out_specs=[row(d), row(mw)],
        out_shape=[jax.ShapeDtypeStruct((t, d), F32), jax.ShapeDtypeStruct((t, mw), F32)],
        compiler_params=_params("parallel"),
        name="outproj",
    )(x, o_fox, o_ret, w_out, g_cross, w_cq)


def _xattn_kernel(x_ref, q_ref, mk_ref, mv_ref, wo_ref, o_ref, *, group):
    tt, mw = q_ref.shape
    n_keys = mk_ref.shape[0]
    q = q_ref[...]
    lane = lax.broadcasted_iota(jnp.int32, (tt, mw), 1)
    qs = jnp.concatenate([jnp.where(lane // HEAD_DIM == h, q, 0.0) for h in range(N_MEM_HEADS)], axis=0)
    s = _dot_nt(qs.astype(BF16), mk_ref[...].astype(BF16))
    if group is not None:
        tok_b = (lax.broadcasted_iota(jnp.int32, s.shape, 0) % tt) // group[0]
        key_b = lax.broadcasted_iota(jnp.int32, s.shape, 1) // group[1]
        s = jnp.where(tok_b == key_b, s, -jnp.inf)
    e = jnp.exp(s - jnp.max(s, axis=1, keepdims=True))
    p = (e / jnp.sum(e, axis=1, keepdims=True)).astype(BF16)
    pv = _dot(p, mv_ref[...].astype(BF16))
    o = jnp.zeros((tt, mw), F32)
    for h in range(N_MEM_HEADS):
        o = o + jnp.where(lane // HEAD_DIM == h, pv[h * tt:(h + 1) * tt], 0.0)
    o_ref[...] = x_ref[...] + _dot(o.astype(BF16), wo_ref[...])
    del n_keys


def _xattn(x, q, mk, mv, w_co, *, tt, keys_per_block, mem_index, group):
    t, d = x.shape
    mw = q.shape[1]
    row = lambda width: pl.BlockSpec((tt, width), lambda i: (i, 0))
    mem = pl.BlockSpec((keys_per_block, mw), lambda i: (mem_index(i), 0))
    return pl.pallas_call(
        functools.partial(_xattn_kernel, group=group),
        grid=(t // tt,),
        in_specs=[row(d), row(mw), mem, mem, _resident(w_co.shape)],
        out_specs=row(d),
        out_shape=jax.ShapeDtypeStruct((t, d), F32),
        compiler_params=_params("parallel"),
        name="cross_attention",
    )(x, q, mk, mv, w_co)


def _memkv_kernel(m_ref, g_ref, wk_ref, wv_ref, k_ref, v_ref):
    h = _rms(m_ref[...], g_ref[...]).astype(BF16)
    k_ref[...] = _dot(h, wk_ref[...])
    v_ref[...] = _dot(h, wv_ref[...])


def _memkv(mem, g, wk, wv, *, tm):
    t, d = mem.shape
    mw = wk.shape[1]
    row = lambda width: pl.BlockSpec((tm, width), lambda i: (i, 0))
    return pl.pallas_call(
        _memkv_kernel,
        grid=(t // tm,),
        in_specs=[row(d), _resident((1, d)), _resident(wk.shape), _resident(wv.shape)],
        out_specs=[row(mw), row(mw)],
        out_shape=[jax.ShapeDtypeStruct((t, mw), F32)] * 2,
        compiler_params=_params("parallel"),
        name="memory_kv",
    )(mem, g, wk, wv)


def _rope_tables(pos):
    half = HEAD_DIM // 2
    inv = ROPE_THETA ** (-jnp.arange(half, dtype=F32) / half)
    ang = pos.astype(F32)[:, None] * inv[None, :]
    cos = jnp.cos(ang)
    sin = jnp.sin(ang)
    return (jnp.tile(jnp.concatenate([cos, cos], axis=-1), (1, N_HEADS)),
            jnp.tile(jnp.concatenate([-sin, sin], axis=-1), (1, N_HEADS)))


def _token_tile(t):
    return 512 if t % 512 == 0 else t


def kernel(x_prompt, x_sample, mem_prompt, cache_fox_k, cache_fox_v, cache_fox_logf, state_ret, cache_mem_k, cache_mem_v, page_table, g_ffn1, w1_gate, w1_up, w1_down, g_mix, w_in, b_f, g_ret, w_out, g_cross, g_mem, w_cq, w_ck, w_cv, w_co, g_ffn2, w2_gate, w2_up, w2_down, g_final):
    depth = w_in.shape[0]
    assert depth == 1, "single-layer trunk"
    b, s, d = x_prompt.shape
    db, ds, _ = x_sample.shape
    n_mem = mem_prompt.shape[1]
    page = cache_fox_k.shape[2]
    past_len = page_table.shape[1] * page
    w = GROUP_WIDTH
    assert s % RET_CHUNK == 0 and ds <= RET_CHUNK

    row = lambda a: a.reshape(1, -1)
    bf = lambda a: a.astype(BF16)
    wi = w_in[0]
    fox_end = 3 * w
    wfox = bf(wi[:, :fox_end])
    wf = bf(jnp.pad(wi[:, fox_end:fox_end + N_HEADS], ((0, 0), (0, LANES - N_HEADS))))
    bfp = jnp.pad(row(b_f[0]), ((0, 0), (0, LANES - N_HEADS)))
    ret0 = fox_end + N_HEADS
    wq_r, wk_r, wv_r, wg_r = (wi[:, ret0 + i * w:ret0 + (i + 1) * w] for i in range(4))
    lane = jnp.arange(w)
    swap = jnp.where(lane % HEAD_DIM < HEAD_DIM // 2, lane + HEAD_DIM // 2, lane - HEAD_DIM // 2)
    wret = bf(jnp.concatenate([wq_r, wq_r[:, swap], wk_r, wk_r[:, swap], wv_r, wg_r], axis=1))
    w1 = (bf(w1_gate[0]), bf(w1_up[0]), bf(w1_down[0]))
    w2 = (bf(w2_gate[0]), bf(w2_up[0]), bf(w2_down[0]))
    wo, wcq, wck, wcv, wco = bf(w_out[0]), bf(w_cq[0]), bf(w_ck[0]), bf(w_cv[0]), bf(w_co[0])
    gf = row(g_final)

    def trunk_in(x, pos):
        tm = _token_tile(x.shape[0])
        x1 = _ffn(x, row(g_ffn1[0]), *w1, gf, final_norm=False, tm=tm)
        cos, sin = _rope_tables(pos)
        return (x1,) + tuple(_inproj(x1, row(g_mix[0]), wfox, wf, bfp, wret, cos, sin, tm=tm))

    def trunk_out(x1, o_fox, o_ret, mk, mv, *, tt, keys_per_block, mem_index, group):
        tm = _token_tile(x1.shape[0])
        x2, qc = _outproj(x1, o_fox, o_ret, wo, row(g_cross[0]), wcq, tm=tm)
        x3 = _xattn(x2, qc, mk, mv, wco, tt=tt, keys_per_block=keys_per_block, mem_index=mem_index, group=group)
        return _ffn(x3, row(g_ffn2[0]), *w2, gf, final_norm=True, tm=tm)

    xp1, qa, ka, va, kab, vab, lf, qr, kr, vr, gr = trunk_in(x_prompt.reshape(b * s, d), jnp.arange(s, dtype=jnp.int32))
    tq = 256
    fcol, frow = _cumsum(jnp.pad(lf.reshape(b, s, N_HEADS), ((0, 0), (0, 0), (0, LANES - N_HEADS))), tc=512)
    o_fox = _fox_prompt(qa.reshape(b, s, w), kab.reshape(b, s, w), vab.reshape(b, s, w), fcol,
                        frow.reshape(b, N_HEADS, s // tq, tq), tq=tq)
    st_zero = jnp.zeros((b, w // LANES, LANES, LANES), F32)
    o_ret, st_p = _retention(qr.reshape(b, s, w), kr.reshape(b, s, w), vr.reshape(b, s, w), gr.reshape(b, s, w),
                             row(g_ret[0]), st_zero, _ret_tables(RET_CHUNK), n_chunks=4)
    mk, mv = _memkv(mem_prompt.reshape(b * n_mem, d), row(g_mem[0]), wck, wcv, tm=_token_tile(b * n_mem))
    tt = 512
    y_prompt = trunk_out(xp1, o_fox.reshape(b * s, w), o_ret.reshape(b * s, w), mk, mv, tt=tt,
                         keys_per_block=n_mem, mem_index=lambda i: i // (s // tt), group=None)

    pos_s = past_len + jnp.arange(ds, dtype=jnp.int32)
    xs1, qa_s, ka_s, va_s, _, _, lf_s, qr_s, kr_s, vr_s, gr_s = trunk_in(x_sample.reshape(db * ds, d), jnp.tile(pos_s, db))
    feature_major = lambda c: jnp.transpose(c, (0, 2, 3, 1)).reshape(c.shape[0], w, page)
    new_page = lambda a, width: jnp.pad(jnp.transpose(a.reshape(db, ds, width), (0, 2, 1)),
                                        ((0, 0), (0, 0), (0, page - ds)))
    o_fox_s = _fox_sample(page_table, qa_s.astype(F32).reshape(db, ds, w), new_page(ka_s, w), new_page(va_s, w),
                          new_page(lf_s, N_HEADS), feature_major(cache_fox_k[0]), feature_major(cache_fox_v[0]),
                          jnp.transpose(cache_fox_logf[0], (0, 2, 1)), pp=16)
    pad_tok = lambda a: jnp.pad(a.reshape(db, ds, w), ((0, 0), (0, RET_CHUNK - ds), (0, 0)))
    o_ret_s, st_s = _retention(pad_tok(qr_s), pad_tok(kr_s), pad_tok(vr_s), pad_tok(gr_s), row(g_ret[0]),
                               _blockdiag_states(state_ret[0]), _ret_tables(ds), n_chunks=1)
    bb = 8
    y_sample = trunk_out(xs1, bf(o_fox_s.reshape(db * ds, w)), o_ret_s[:, :ds].reshape(db * ds, w),
                         cache_mem_k[0].reshape(db * n_mem, -1), cache_mem_v[0].reshape(db * n_mem, -1),
                         tt=bb * ds, keys_per_block=bb * n_mem, mem_index=lambda i: i, group=(ds, n_mem))

    mem_heads = lambda a: a.reshape(1, b, n_mem, N_MEM_HEADS, -1)
    return (y_prompt.reshape(b, s, d), y_sample.reshape(db, ds, d),
            ka.reshape(1, b, s, N_HEADS, HEAD_DIM), va.reshape(1, b, s, N_HEADS, HEAD_DIM),
            lf.reshape(1, b, s, N_HEADS), _head_states(st_p)[None], mem_heads(mk), mem_heads(mv),
            ka_s.reshape(1, db, ds, N_HEADS, HEAD_DIM), va_s.reshape(1, db, ds, N_HEADS, HEAD_DIM),
            lf_s.reshape(1, db, ds, N_HEADS), _head_states(st_s)[None])
```

```python
import functools

import jax
import jax.numpy as jnp
from jax import lax
from jax.experimental import pallas as pl
from jax.experimental.pallas import tpu as pltpu

F32 = jnp.float32
BF16 = jnp.bfloat16

HEAD_DIM = 64
N_HEADS = 8
GROUP_WIDTH = N_HEADS * HEAD_DIM
N_MEM_HEADS = 4
ROPE_THETA = 10000.0
EPS = 1e-6
FFN_RES = 0.5
Q_SCALE = HEAD_DIM ** -0.5
RET_CHUNK = 128
LANES = 128
VMEM_LIMIT = 56 * 1024 * 1024


def _params(*sem):
    return pltpu.CompilerParams(dimension_semantics=sem, vmem_limit_bytes=VMEM_LIMIT)


def _resident(shape):
    nd = len(shape)
    return pl.BlockSpec(shape, lambda *_: (0,) * nd, pipeline_mode=pl.Buffered(1))


def _dot(a, b):
    return jnp.dot(a, b, preferred_element_type=F32)


def _dot_nt(a, b):
    return lax.dot_general(a, b, (((1,), (1,)), ((), ())), preferred_element_type=F32)


def _rms(x, g):
    return x * lax.rsqrt(jnp.mean(x * x, axis=-1, keepdims=True) + EPS) * g


def _silu(x):
    return x * jax.nn.sigmoid(x)


def _split3(x):
    hi = x.astype(BF16)
    r1 = x - hi.astype(F32)
    mid = r1.astype(BF16)
    lo = (r1 - mid.astype(F32)).astype(BF16)
    return hi, mid, lo


def _dot_exact01(x, w01, w_left=False):
    mm = (lambda part: _dot(w01, part)) if w_left else (lambda part: _dot(part, w01))
    hi, mid, lo = _split3(x)
    return mm(hi) + mm(mid) + mm(lo)


def _ffn_kernel(x_ref, g_ref, wg_ref, wu_ref, wd_ref, gf_ref, o_ref, *, chunks, final_norm):
    x = x_ref[...]
    h = _rms(x, g_ref[...]).astype(BF16)
    acc = jnp.zeros_like(x)
    start = 0
    for width in chunks:
        sl = slice(start, start + width)
        a = _silu(_dot(h, wg_ref[:, sl])) * _dot(h, wu_ref[:, sl])
        acc = acc + _dot(a.astype(BF16), wd_ref[sl, :])
        start += width
    y = x + FFN_RES * acc
    if final_norm:
        y = _rms(y, gf_ref[...])
    o_ref[...] = y


def _ffn(x, g, wg, wu, wd, gf, *, final_norm, tm):
    t, d = x.shape
    d_ff = wg.shape[1]
    chunks = [512] * (d_ff // 512)
    if d_ff % 512:
        chunks.append(d_ff % 512)
    row = pl.BlockSpec((tm, d), lambda i: (i, 0))
    return pl.pallas_call(
        functools.partial(_ffn_kernel, chunks=tuple(chunks), final_norm=final_norm),
        grid=(t // tm,),
        in_specs=[row, _resident((1, d)), _resident(wg.shape), _resident(wu.shape),
                  _resident(wd.shape), _resident((1, d))],
        out_specs=row,
        out_shape=jax.ShapeDtypeStruct((t, d), F32),
        compiler_params=_params("parallel"),
        name="ffn",
    )(x, g, wg, wu, wd, gf)


def _inproj_kernel(x_ref, g_ref, wfox_ref, wf_ref, bf_ref, wret_ref, cos_ref, sin_ref,
                   qa_ref, ka_ref, va_ref, kab_ref, vab_ref, lf_ref, qr_ref, kr_ref, vr_ref, gr_ref):
    h = _rms(x_ref[...], g_ref[...]).astype(BF16)
    w = GROUP_WIDTH
    qa_ref[...] = (_dot(h, wfox_ref[:, 0:w]) * Q_SCALE).astype(BF16)
    ka = _dot(h, wfox_ref[:, w:2 * w])
    ka_ref[...] = ka
    kab_ref[...] = ka.astype(BF16)
    va = _dot(h, wfox_ref[:, 2 * w:3 * w])
    va_ref[...] = va
    vab_ref[...] = va.astype(BF16)
    z = _dot(h, wf_ref[...]) + bf_ref[...]
    lf_ref[...] = (jnp.minimum(z, 0.0) - jnp.log1p(jnp.exp(-jnp.abs(z))))[:, :N_HEADS]
    cos = cos_ref[...]
    sin = sin_ref[...]
    qr_ref[...] = (_dot(h, wret_ref[:, 0:w]) * cos + _dot(h, wret_ref[:, w:2 * w]) * sin).astype(BF16)
    kr = _dot(h, wret_ref[:, 2 * w:3 * w]) * cos + _dot(h, wret_ref[:, 3 * w:4 * w]) * sin
    kr_ref[...] = (kr * Q_SCALE).astype(BF16)
    vr_ref[...] = _dot(h, wret_ref[:, 4 * w:5 * w]).astype(BF16)
    gr_ref[...] = _dot(h, wret_ref[:, 5 * w:6 * w])


def _inproj(x, g, wfox, wf, bf, wret, cos, sin, *, tm):
    t, d = x.shape
    w = GROUP_WIDTH
    n_pos = cos.shape[0] // tm
    row = lambda width: pl.BlockSpec((tm, width), lambda i: (i, 0))
    tab = pl.BlockSpec((tm, w), lambda i: (i % n_pos, 0))
    sds = lambda width, dt: jax.ShapeDtypeStruct((t, width), dt)
    return pl.pallas_call(
        _inproj_kernel,
        grid=(t // tm,),
        in_specs=[row(d), _resident((1, d)), _resident(wfox.shape), _resident(wf.shape),
                  _resident(bf.shape), _resident(wret.shape), tab, tab],
        out_specs=[row(w), row(w), row(w), row(w), row(w), row(N_HEADS), row(w), row(w), row(w), row(w)],
        out_shape=[sds(w, BF16), sds(w, F32), sds(w, F32), sds(w, BF16), sds(w, BF16), sds(N_HEADS, F32),
                   sds(w, BF16), sds(w, BF16), sds(w, BF16), sds(w, F32)],
        compiler_params=_params("parallel"),
        name="inproj",
    )(x, g, wfox, wf, bf, wret, cos, sin)


def _inproj_prompt_kernel(x_ref, g_ref, wt_ref, wk_ref, wf_ref, bf_ref, wret_ref, cos_ref, sin_ref,
                          qt_ref, kt_ref, vt_ref, kb_ref, vtb_ref, lf_ref, lft_ref, qr_ref, kr_ref, vr_ref, gr_ref):
    h = _rms(x_ref[...], g_ref[...]).astype(BF16)
    w = GROUP_WIDTH
    qt_ref[...] = (_dot_nt(wt_ref[0:w, :], h) * Q_SCALE).astype(BF16)
    kt_ref[...] = _dot_nt(wt_ref[w:2 * w, :], h)
    vt = _dot_nt(wt_ref[2 * w:3 * w, :], h)
    vt_ref[...] = vt
    tk = vtb_ref.shape[-1]
    for c in range(vtb_ref.shape[0]):
        vtb_ref[c] = vt[:, c * tk:(c + 1) * tk].astype(BF16)
    kb_ref[...] = _dot(h, wk_ref[...]).astype(BF16)
    z = _dot(h, wf_ref[...]) + bf_ref[...]
    lane = lax.broadcasted_iota(jnp.int32, z.shape, 1)
    lf = jnp.where(lane < N_HEADS, jnp.minimum(z, 0.0) - jnp.log1p(jnp.exp(-jnp.abs(z))), 0.0)
    lf_ref[...] = lf
    lft_ref[...] = lf.T[:N_HEADS, :]
    cos = cos_ref[...]
    sin = sin_ref[...]
    qr_ref[...] = (_dot(h, wret_ref[:, 0:w]) * cos + _dot(h, wret_ref[:, w:2 * w]) * sin).astype(BF16)
    kr = _dot(h, wret_ref[:, 2 * w:3 * w]) * cos + _dot(h, wret_ref[:, 3 * w:4 * w]) * sin
    kr_ref[...] = (kr * Q_SCALE).astype(BF16)
    vr_ref[...] = _dot(h, wret_ref[:, 4 * w:5 * w]).astype(BF16)
    gr_ref[...] = _dot(h, wret_ref[:, 5 * w:6 * w])


def _inproj_prompt(x, g, wt, wk, wf, bf, wret, cos, sin, *, batch, tm, tk):
    t, d = x.shape
    w = GROUP_WIDTH
    s = t // batch
    n_s = s // tm
    row = lambda width: pl.BlockSpec((tm, width), lambda i: (i, 0))
    tab = pl.BlockSpec((tm, w), lambda i: (i % n_s, 0))
    fmajor = lambda rows: pl.BlockSpec((None, rows, tm), lambda i: (i // n_s, 0, i % n_s))
    sds = jax.ShapeDtypeStruct
    return pl.pallas_call(
        _inproj_prompt_kernel,
        grid=(t // tm,),
        in_specs=[row(d), _resident((1, d)), _resident(wt.shape), _resident(wk.shape), _resident(wf.shape),
                  _resident(bf.shape), _resident(wret.shape), tab, tab],
        out_specs=[fmajor(w), fmajor(w), fmajor(w), row(w),
                   pl.BlockSpec((None, tm // tk, w, tk), lambda i: (i // n_s, i % n_s, 0, 0)),
                   row(LANES), fmajor(N_HEADS), row(w), row(w), row(w), row(w)],
        out_shape=[sds((batch, w, s), BF16), sds((batch, w, s), F32), sds((batch, w, s), F32), sds((t, w), BF16),
                   sds((batch, s // tk, w, tk), BF16), sds((t, LANES), F32), sds((batch, N_HEADS, s), F32),
                   sds((t, w), BF16), sds((t, w), BF16), sds((t, w), BF16), sds((t, w), F32)],
        compiler_params=_params("parallel"),
        name="inproj_prompt",
    )(x, g, wt, wk, wf, bf, wret, cos, sin)


BIAS_TERMS = 3


def _cumsum_kernel(lf_ref, lft_ref, fb_ref, frow_ref, carry_col, carry_row, *, tc):
    @pl.when(pl.program_id(1) == 0)
    def _():
        carry_col[...] = jnp.zeros_like(carry_col)
        carry_row[...] = jnp.zeros_like(carry_row)

    r = lax.broadcasted_iota(jnp.int32, (tc, tc), 0)
    c = lax.broadcasted_iota(jnp.int32, (tc, tc), 1)
    lower = jnp.where(c <= r, 1.0, 0.0).astype(BF16)
    fcol = _dot_exact01(lf_ref[...], lower, w_left=True) + carry_col[...]
    carry_col[...] = fcol[tc - 1:tc, :]
    rr = lax.broadcasted_iota(jnp.int32, (LANES, LANES), 0)
    cc = lax.broadcasted_iota(jnp.int32, (LANES, LANES), 1)
    fb = jnp.zeros((tc, LANES), F32)
    for j, term in enumerate(_split3(-fcol)):
        place = jnp.where((cc == BIAS_TERMS * rr + j) & (rr < N_HEADS), 1.0, 0.0).astype(BF16)
        fb = fb + _dot(term, place)
    fb_ref[...] = fb.astype(BF16)
    upper = jnp.where(r <= c, 1.0, 0.0).astype(BF16)
    frow = _dot_exact01(lft_ref[...], upper) + carry_row[...]
    frow_ref[...] = frow
    carry_row[...] = frow[:, tc - 1:tc]


def _cumsum(lf_pad, lf_t, *, tc):
    b, s, _ = lf_pad.shape
    col = pl.BlockSpec((None, tc, LANES), lambda i, j: (i, j, 0))
    rowspec = pl.BlockSpec((None, N_HEADS, tc), lambda i, j: (i, 0, j))
    return pl.pallas_call(
        functools.partial(_cumsum_kernel, tc=tc),
        grid=(b, s // tc),
        in_specs=[col, rowspec],
        out_specs=[col, rowspec],
        out_shape=[jax.ShapeDtypeStruct((b, s, LANES), BF16), jax.ShapeDtypeStruct((b, N_HEADS, s), F32)],
        scratch_shapes=[pltpu.VMEM((1, LANES), F32), pltpu.VMEM((N_HEADS, 1), F32)],
        compiler_params=_params("parallel", "arbitrary"),
        name="logf_cumsum",
    )(lf_pad, lf_t)


def _fox_prompt_kernel(qt_ref, k_ref, fb_ref, vt_ref, frow_ref, o_ref, *, tq):
    pg = pl.program_id(1)
    qi = pl.program_id(2)
    pairs = qt_ref.shape[0] // LANES
    heads = 2 * pairs
    row = lax.broadcasted_iota(jnp.int32, (LANES, tq), 0)
    qw, fq = [], []
    for p in range(pairs):
        qt = qt_ref[p * LANES:(p + 1) * LANES, :].astype(F32)
        both = []
        for h in (0, 1):
            head = 2 * (pg * pairs + p) + h
            own = (row < HEAD_DIM) if h == 0 else (row >= HEAD_DIM)
            first = BIAS_TERMS * head
            ones = jnp.where((row >= first) & (row < first + BIAS_TERMS), 1.0, 0.0)
            both.append(jnp.concatenate([jnp.where(own, qt, 0.0), ones], axis=0).astype(BF16))
            fq.append(frow_ref[pl.ds(head, 1), :])
        qw.append(jnp.concatenate(both, axis=1))
    fq = jnp.concatenate(fq, axis=1)

    def scores(kj):
        start = pl.multiple_of(kj * tq, tq)
        bias = fb_ref[pl.ds(start, tq), :]
        return jnp.concatenate(
            [_dot(jnp.concatenate([k_ref[pl.ds(start, tq), p * LANES:(p + 1) * LANES], bias], axis=1), qw[p])
             for p in range(pairs)], axis=1)

    def consume(kj, t, stats, masked):
        m_old, l_old, acc = stats
        if masked:
            key = lax.broadcasted_iota(jnp.int32, t.shape, 0)
            qry = lax.broadcasted_iota(jnp.int32, t.shape, 1) & (tq - 1)
            t = jnp.where(key <= qry, t, -jnp.inf)
        m_new = jnp.maximum(m_old, jnp.max(t, axis=0, keepdims=True) + fq)
        p = jnp.exp(t + (fq - m_new))
        alpha = jnp.exp(m_old - m_new)
        pb = p.astype(BF16)
        pv = jnp.concatenate([_dot(vt_ref[kj, h * HEAD_DIM:(h + 1) * HEAD_DIM, :], pb[:, h * tq:(h + 1) * tq])
                              for h in range(heads)], axis=1)
        return m_new, alpha * l_old + jnp.sum(p, axis=0, keepdims=True), alpha * acc + pv

    def body(kj, carry):
        t_cur, stats = carry
        t_next = scores(kj + 1)
        return t_next, consume(kj, t_cur, stats, False)

    init = (jnp.full((1, heads * tq), -1e30, F32), jnp.zeros((1, heads * tq), F32),
            jnp.zeros((HEAD_DIM, heads * tq), F32))
    t_last, stats = lax.fori_loop(0, qi, body, (scores(0), init))
    _, l_fin, acc = consume(qi, t_last, stats, True)
    out = acc / l_fin
    o_ref[...] = jnp.concatenate([out[:, h * tq:(h + 1) * tq] for h in range(heads)], axis=0).T.astype(o_ref.dtype)


def _fox_prompt(qt, k, fb, vtb, frow, *, tq, pairs):
    b, w, s = qt.shape
    wb = pairs * LANES
    nk = s // tq
    assert tq & (tq - 1) == 0
    return pl.pallas_call(
        functools.partial(_fox_prompt_kernel, tq=tq),
        grid=(b, w // wb, nk),
        in_specs=[pl.BlockSpec((None, wb, tq), lambda i, p, j: (i, p, j)),
                  pl.BlockSpec((None, s, wb), lambda i, p, j: (i, 0, p)),
                  pl.BlockSpec((None, s, LANES), lambda i, p, j: (i, 0, 0)),
                  pl.BlockSpec((None, nk, wb, tq), lambda i, p, j: (i, 0, p, 0)),
                  pl.BlockSpec((None, N_HEADS, tq), lambda i, p, j: (i, 0, j))],
        out_specs=pl.BlockSpec((None, tq, wb), lambda i, p, j: (i, j, p)),
        out_shape=jax.ShapeDtypeStruct((b, s, w), BF16),
        compiler_params=_params("parallel", "parallel", "arbitrary"),
        name="fox_prompt",
    )(qt, k, fb, vtb, frow)


def _ret_kernel(q_ref, k_ref, v_ref, gate_ref, gret_ref, st0_ref, dec_ref, cross_ref, kdec_ref, gam_ref,
                o_ref, st_ref, state, *, n_chunks):
    c = RET_CHUNK
    j = pl.program_id(1)

    @pl.when(j == 0)
    def _():
        state[...] = st0_ref[...]

    lane = lax.broadcasted_iota(jnp.int32, (c, LANES), 1)
    lo = lane < HEAD_DIM
    r = lax.broadcasted_iota(jnp.int32, (LANES, LANES), 0)
    cc = lax.broadcasted_iota(jnp.int32, (LANES, LANES), 1)
    same_head = (r < HEAD_DIM) == (cc < HEAD_DIM)
    mean_w = jnp.where(same_head, 1.0 / HEAD_DIM, 0.0).astype(BF16)
    for ci in range(n_chunks):
        rows = slice(ci * c, (ci + 1) * c)
        outs = []
        for g in range(GROUP_WIDTH // LANES):
            ls = slice(g * LANES, (g + 1) * LANES)
            qg = q_ref[rows, ls]
            kg = k_ref[rows, ls]
            vg = v_ref[rows, ls]
            zero = jnp.zeros_like(qg)
            s0 = _dot_nt(jnp.where(lo, qg, zero), kg) * dec_ref[2 * g]
            s1 = _dot_nt(jnp.where(lo, zero, qg), kg) * dec_ref[2 * g + 1]
            o_inner = jnp.where(lo, _dot(s0.astype(BF16), vg), _dot(s1.astype(BF16), vg))
            st = state[g]
            o = o_inner + _dot(qg, st.astype(BF16)) * cross_ref[:, ls]
            kd = kg.astype(F32) * kdec_ref[:, ls]
            upd = _dot(kd.T.astype(BF16), vg)
            state[g] = gam_ref[g] * st + jnp.where(same_head, upd, 0.0)
            o2 = o * o
            hi = o2.astype(BF16)
            ms = _dot(hi, mean_w) + _dot((o2 - hi.astype(F32)).astype(BF16), mean_w)
            of = o * lax.rsqrt(ms + EPS)
            outs.append(((of * gret_ref[:, ls]) * _silu(gate_ref[rows, ls])).astype(BF16))
        o_ref[rows, :] = jnp.concatenate(outs, axis=1)

    @pl.when(j == pl.num_programs(1) - 1)
    def _():
        st_ref[...] = state[...]


def _retention(q, k, v, gate, gret, st0, tables, *, n_chunks):
    b, s, w = q.shape
    tr = n_chunks * RET_CHUNK
    groups = w // LANES
    dec, cross, kdec, gam = tables
    tok = pl.BlockSpec((None, tr, w), lambda i, j: (i, j, 0))
    stspec = pl.BlockSpec((None, groups, LANES, LANES), lambda i, j: (i, 0, 0, 0))
    return pl.pallas_call(
        functools.partial(_ret_kernel, n_chunks=n_chunks),
        grid=(b, s // tr),
        in_specs=[tok, tok, tok, tok, _resident((1, w)), stspec, _resident(dec.shape), _resident(cross.shape),
                  _resident(kdec.shape), _resident(gam.shape)],
        out_specs=[tok, stspec],
        out_shape=[jax.ShapeDtypeStruct((b, s, w), BF16), jax.ShapeDtypeStruct((b, groups, LANES, LANES), F32)],
        scratch_shapes=[pltpu.VMEM((groups, LANES, LANES), F32)],
        compiler_params=_params("parallel", "arbitrary"),
        name="retention",
    )(q, k, v, gate, gret, st0, dec, cross, kdec, gam)


def _ret_tables(chunk_len):
    c = RET_CHUNK
    log_gamma = jnp.log(1.0 - 2.0 ** (-5.0 - jnp.arange(N_HEADS, dtype=F32)))
    idx = jnp.arange(c, dtype=F32)
    diff = idx[:, None] - idx[None, :]
    dec = jnp.where(diff >= 0, jnp.exp(jnp.maximum(diff, 0.0)[None] * log_gamma[:, None, None]), 0.0)
    cross = jnp.exp((idx + 1.0)[:, None] * log_gamma[None, :])
    kdec = jnp.exp((chunk_len - 1.0 - idx)[:, None] * log_gamma[None, :])
    sdec = jnp.exp(chunk_len * log_gamma)
    head_of_lane = jnp.arange(GROUP_WIDTH) // HEAD_DIM
    row_head = head_of_lane.reshape(GROUP_WIDTH // LANES, LANES)
    same = (jnp.arange(LANES)[:, None] // HEAD_DIM) == (jnp.arange(LANES)[None, :] // HEAD_DIM)
    gam = jnp.where(same[None], sdec[row_head][:, :, None], 0.0)
    return dec, cross[:, head_of_lane], kdec[:, head_of_lane], gam


def _blockdiag_states(st):
    b = st.shape[0]
    st = st.reshape(b, N_HEADS // 2, 2, HEAD_DIM, HEAD_DIM)
    z = jnp.zeros_like(st[:, :, 0])
    top = jnp.concatenate([st[:, :, 0], z], axis=-1)
    bot = jnp.concatenate([z, st[:, :, 1]], axis=-1)
    return jnp.concatenate([top, bot], axis=-2)


def _head_states(bd):
    b = bd.shape[0]
    a = bd[:, :, :HEAD_DIM, :HEAD_DIM]
    d = bd[:, :, HEAD_DIM:, HEAD_DIM:]
    return jnp.stack([a, d], axis=2).reshape(b, N_HEADS, HEAD_DIM, HEAD_DIM)


def _fox_sample_kernel(pt_ref, q_ref, kn_ref, vn_ref, lfn_ref, *rest, pp):
    k_refs = rest[:pp]
    v_refs = rest[pp:2 * pp]
    lf_refs = rest[2 * pp:3 * pp]
    o_ref = rest[3 * pp]
    qbd_s, m_s, l_s, acc_s, carry_s, fq_s = rest[3 * pp + 1:]
    del pt_ref
    j = pl.program_id(1)
    n_new, w = q_ref.shape
    rows = n_new * N_HEADS
    page = k_refs[0].shape[1]
    sub = lax.broadcasted_iota(jnp.int32, (N_HEADS, w), 0)
    own = (lax.broadcasted_iota(jnp.int32, (N_HEADS, w), 1) // HEAD_DIM) == sub
    tr = lax.broadcasted_iota(jnp.int32, (page, page), 0)
    tc = lax.broadcasted_iota(jnp.int32, (page, page), 1)
    tile_rows = lambda x: jnp.concatenate([x] * n_new, axis=0)

    @pl.when(j == 0)
    def _():
        q = q_ref[...]
        qbd = jnp.concatenate([jnp.where(own, jnp.broadcast_to(q[t:t + 1, :], (N_HEADS, w)), 0.0)
                               for t in range(n_new)], axis=0).astype(BF16)
        qbd_s[...] = qbd
        incl = jnp.where(tr <= tc, 1.0, 0.0).astype(BF16)
        cum = _dot_exact01(lfn_ref[...], incl)
        fq = jnp.concatenate([cum[:, t:t + 1] for t in range(n_new)], axis=0)
        fq_s[...] = fq
        s = _dot(qbd, kn_ref[...].astype(BF16))
        t_of_row = lax.broadcasted_iota(jnp.int32, (rows, page), 0) // N_HEADS
        m_idx = lax.broadcasted_iota(jnp.int32, (rows, page), 1)
        logits = jnp.where(m_idx <= t_of_row, s + fq - tile_rows(cum), -jnp.inf)
        m0 = jnp.max(logits, axis=1, keepdims=True)
        p = jnp.exp(logits - m0)
        m_s[...] = m0
        l_s[...] = jnp.sum(p, axis=1, keepdims=True)
        acc_s[...] = _dot_nt(p.astype(BF16), vn_ref[...].astype(BF16))
        carry_s[...] = jnp.zeros_like(carry_s)

    lfs = [lf_refs[i][...] for i in range(pp)]
    strict = jnp.where(tr > tc, 1.0, 0.0).astype(BF16)
    suffix = _dot_exact01(jnp.concatenate(lfs, axis=0), strict)
    carry = carry_s[...]
    biases = []
    for i in range(pp):
        d_i = suffix[i * N_HEADS:(i + 1) * N_HEADS]
        biases.append(tile_rows(d_i + carry))
        carry = carry + d_i[:, 0:1] + lfs[i][:, 0:1]
    carry_s[...] = carry
    kcat = jnp.concatenate([k_refs[i][...].astype(BF16) for i in range(pp)], axis=1)
    t = _dot(qbd_s[...], kcat) + jnp.concatenate(biases, axis=1)
    fq = fq_s[...]
    m_old = m_s[...]
    m_new = jnp.maximum(m_old, jnp.max(t, axis=1, keepdims=True) + fq)
    p = jnp.exp(t + (fq - m_new))
    alpha = jnp.exp(m_old - m_new)
    vcat = jnp.concatenate([v_refs[i][...].astype(BF16) for i in range(pp)], axis=1)
    l_new = alpha * l_s[...] + jnp.sum(p, axis=1, keepdims=True)
    acc = alpha * acc_s[...] + _dot_nt(p.astype(BF16), vcat)
    m_s[...] = m_new
    l_s[...] = l_new
    acc_s[...] = acc

    @pl.when(j == pl.num_programs(1) - 1)
    def _():
        out = acc / l_new
        o_ref[...] = jnp.concatenate(
            [jnp.sum(jnp.where(own, out[t * N_HEADS:(t + 1) * N_HEADS], 0.0), axis=0, keepdims=True)
             for t in range(n_new)], axis=0)


def _fox_sample(page_table, q, k_new_t, v_new_t, lf_new_t, cache_k_t, cache_v_t, cache_lf_t, *, pp):
    b, n_new, w = q.shape
    n_pages = page_table.shape[1]
    page = cache_k_t.shape[2]
    rows = n_new * N_HEADS
    per_row = lambda shape: pl.BlockSpec((None,) + shape, lambda i, j, pt: (i, 0, 0))

    def paged(shape, slot):
        return pl.BlockSpec((None,) + shape, lambda i, j, pt: (pt[i, n_pages - 1 - (j * pp + slot)], 0, 0))

    kv = [paged((w, page), s) for s in range(pp)]
    lf = [paged((N_HEADS, page), s) for s in range(pp)]
    grid_spec = pltpu.PrefetchScalarGridSpec(
        num_scalar_prefetch=1,
        grid=(b, n_pages // pp),
        in_specs=[per_row((n_new, w)), per_row((w, page)), per_row((w, page)), per_row((N_HEADS, page))] + kv + kv + lf,
        out_specs=per_row((n_new, w)),
        scratch_shapes=[pltpu.VMEM((rows, w), BF16), pltpu.VMEM((rows, 1), F32), pltpu.VMEM((rows, 1), F32),
                        pltpu.VMEM((rows, w), F32), pltpu.VMEM((N_HEADS, 1), F32), pltpu.VMEM((rows, 1), F32)],
    )
    return pl.pallas_call(
        functools.partial(_fox_sample_kernel, pp=pp),
        grid_spec=grid_spec,
        out_shape=jax.ShapeDtypeStruct(q.shape, F32),
        compiler_params=_params("parallel", "arbitrary"),
        name="fox_sample_paged",
    )(page_table, q, k_new_t, v_new_t, lf_new_t, *([cache_k_t] * pp), *([cache_v_t] * pp), *([cache_lf_t] * pp))


def _outproj_kernel(x_ref, of_ref, or_ref, wo_ref, g_ref, wq_ref, x2_ref, q_ref):
    w = GROUP_WIDTH
    x2 = x_ref[...] + _dot(of_ref[...], wo_ref[0:w, :]) + _dot(or_ref[...], wo_ref[w:2 * w, :])
    x2_ref[...] = x2
    q_ref[...] = _dot(_rms(x2, g_ref[...]).astype(BF16), wq_ref[...]) * Q_SCALE


def _outproj(x, o_fox, o_ret, w_out, g_cross, w_cq, *, tm):
    t, d = x.shape
    w = GROUP_WIDTH
    mw = w_cq.shape[1]
    row = lambda width: pl.BlockSpec((tm, width), lambda i: (i, 0))
    return pl.pallas_call(
        _outproj_kernel,
        grid=(t // tm,),
        in_specs=[row(d), row(w), row(w), _resident(w_out.shape), _resident((1, d)), _resident(w_cq.shape)],
        out_specs=[row(d), row(mw)],
        out_shape=[jax.ShapeDtypeStruct((t, d), F32), jax.ShapeDtypeStruct((t, mw), F32)],
        compiler_params=_params("parallel"),
        name="outproj",
    )(x, o_fox, o_ret, w_out, g_cross, w_cq)


def _xattn_kernel(x_ref, q_ref, mk_ref, mv_ref, wo_ref, o_ref, *, group):
    tt, mw = q_ref.shape
    n_keys = mk_ref.shape[0]
    q = q_ref[...]
    lane = lax.broadcasted_iota(jnp.int32, (tt, mw), 1)
    qs = jnp.concatenate([jnp.where(lane // HEAD_DIM == h, q, 0.0) for h in range(N_MEM_HEADS)], axis=0)
    s = _dot_nt(qs.astype(BF16), mk_ref[...].astype(BF16))
    if group is not None:
        tok_b = (lax.broadcasted_iota(jnp.int32, s.shape, 0) % tt) // group[0]
        key_b = lax.broadcasted_iota(jnp.int32, s.shape, 1) // group[1]
        s = jnp.where(tok_b == key_b, s, -jnp.inf)
    e = jnp.exp(s - jnp.max(s, axis=1, keepdims=True))
    p = (e / jnp.sum(e, axis=1, keepdims=True)).astype(BF16)
    pv = _dot(p, mv_ref[...].astype(BF16))
    o = jnp.zeros((tt, mw), F32)
    for h in range(N_MEM_HEADS):
        o = o + jnp.where(lane // HEAD_DIM == h, pv[h * tt:(h + 1) * tt], 0.0)
    o_ref[...] = x_ref[...] + _dot(o.astype(BF16), wo_ref[...])
    del n_keys


def _xattn(x, q, mk, mv, w_co, *, tt, keys_per_block, mem_index, group):
    t, d = x.shape
    mw = q.shape[1]
    row = lambda width: pl.BlockSpec((tt, width), lambda i: (i, 0))
    mem = pl.BlockSpec((keys_per_block, mw), lambda i: (mem_index(i), 0))
    return pl.pallas_call(
        functools.partial(_xattn_kernel, group=group),
        grid=(t // tt,),
        in_specs=[row(d), row(mw), mem, mem, _resident(w_co.shape)],
        out_specs=row(d),
        out_shape=jax.ShapeDtypeStruct((t, d), F32),
        compiler_params=_params("parallel"),
        name="cross_attention",
    )(x, q, mk, mv, w_co)


def _memkv_kernel(m_ref, g_ref, wk_ref, wv_ref, k_ref, v_ref):
    h = _rms(m_ref[...], g_ref[...]).astype(BF16)
    k_ref[...] = _dot(h, wk_ref[...])
    v_ref[...] = _dot(h, wv_ref[...])


def _memkv(mem, g, wk, wv, *, tm):
    t, d = mem.shape
    mw = wk.shape[1]
    row = lambda width: pl.BlockSpec((tm, width), lambda i: (i, 0))
    return pl.pallas_call(
        _memkv_kernel,
        grid=(t // tm,),
        in_specs=[row(d), _resident((1, d)), _resident(wk.shape), _resident(wv.shape)],
        out_specs=[row(mw), row(mw)],
        out_shape=[jax.ShapeDtypeStruct((t, mw), F32)] * 2,
        compiler_params=_params("parallel"),
        name="memory_kv",
    )(mem, g, wk, wv)


def _rope_tables(pos):
    half = HEAD_DIM // 2
    inv = ROPE_THETA ** (-jnp.arange(half, dtype=F32) / half)
    ang = pos.astype(F32)[:, None] * inv[None, :]
    cos = jnp.cos(ang)
    sin = jnp.sin(ang)
    return (jnp.tile(jnp.concatenate([cos, cos], axis=-1), (1, N_HEADS)),
            jnp.tile(jnp.concatenate([-sin, sin], axis=-1), (1, N_HEADS)))


def _token_tile(t):
    return 512 if t % 512 == 0 else t


def kernel(x_prompt, x_sample, mem_prompt, cache_fox_k, cache_fox_v, cache_fox_logf, state_ret, cache_mem_k, cache_mem_v, page_table, g_ffn1, w1_gate, w1_up, w1_down, g_mix, w_in, b_f, g_ret, w_out, g_cross, g_mem, w_cq, w_ck, w_cv, w_co, g_ffn2, w2_gate, w2_up, w2_down, g_final):
    depth = w_in.shape[0]
    assert depth == 1, "single-layer trunk"
    b, s, d = x_prompt.shape
    db, ds, _ = x_sample.shape
    n_mem = mem_prompt.shape[1]
    page = cache_fox_k.shape[2]
    past_len = page_table.shape[1] * page
    w = GROUP_WIDTH
    assert s % RET_CHUNK == 0 and ds <= RET_CHUNK

    row = lambda a: a.reshape(1, -1)
    bf = lambda a: a.astype(BF16)
    wi = w_in[0]
    fox_end = 3 * w
    wfox = bf(wi[:, :fox_end])
    wf = bf(jnp.pad(wi[:, fox_end:fox_end + N_HEADS], ((0, 0), (0, LANES - N_HEADS))))
    bfp = jnp.pad(row(b_f[0]), ((0, 0), (0, LANES - N_HEADS)))
    ret0 = fox_end + N_HEADS
    wq_r, wk_r, wv_r, wg_r = (wi[:, ret0 + i * w:ret0 + (i + 1) * w] for i in range(4))
    lane = jnp.arange(w)
    swap = jnp.where(lane % HEAD_DIM < HEAD_DIM // 2, lane + HEAD_DIM // 2, lane - HEAD_DIM // 2)
    wret = bf(jnp.concatenate([wq_r, wq_r[:, swap], wk_r, wk_r[:, swap], wv_r, wg_r], axis=1))
    w1 = (bf(w1_gate[0]), bf(w1_up[0]), bf(w1_down[0]))
    w2 = (bf(w2_gate[0]), bf(w2_up[0]), bf(w2_down[0]))
    wo, wcq, wck, wcv, wco = bf(w_out[0]), bf(w_cq[0]), bf(w_ck[0]), bf(w_cv[0]), bf(w_co[0])
    gf = row(g_final)

    def trunk_in(x, pos):
        tm = _token_tile(x.shape[0])
        x1 = _ffn(x, row(g_ffn1[0]), *w1, gf, final_norm=False, tm=tm)
        cos, sin = _rope_tables(pos)
        return (x1,) + tuple(_inproj(x1, row(g_mix[0]), wfox, wf, bfp, wret, cos, sin, tm=tm))

    def trunk_out(x1, o_fox, o_ret, mk, mv, *, tt, keys_per_block, mem_index, group):
        tm = _token_tile(x1.shape[0])
        x2, qc = _outproj(x1, o_fox, o_ret, wo, row(g_cross[0]), wcq, tm=tm)
        x3 = _xattn(x2, qc, mk, mv, wco, tt=tt, keys_per_block=keys_per_block, mem_index=mem_index, group=group)
        return _ffn(x3, row(g_ffn2[0]), *w2, gf, final_norm=True, tm=tm)

    tq = 256
    tm_p = _token_tile(b * s)
    xp1 = _ffn(x_prompt.reshape(b * s, d), row(g_ffn1[0]), *w1, gf, final_norm=False, tm=tm_p)
    cos_p, sin_p = _rope_tables(jnp.arange(s, dtype=jnp.int32))
    wi_t = wi.T
    qt, kt, vt, kab, vtb, lfp, lft, qr, kr, vr, gr = _inproj_prompt(
        xp1, row(g_mix[0]), bf(wi_t[:fox_end]), bf(wi[:, w:2 * w]), wf, bfp, wret, cos_p, sin_p,
        batch=b, tm=tm_p, tk=tq)
    fb, frow = _cumsum(lfp.reshape(b, s, LANES), lft, tc=512)
    o_fox = _fox_prompt(qt, kab.reshape(b, s, w), fb, vtb, frow, tq=tq, pairs=2)
    st_zero = jnp.zeros((b, w // LANES, LANES, LANES), F32)
    o_ret, st_p = _retention(qr.reshape(b, s, w), kr.reshape(b, s, w), vr.reshape(b, s, w), gr.reshape(b, s, w),
                             row(g_ret[0]), st_zero, _ret_tables(RET_CHUNK), n_chunks=4)
    mk, mv = _memkv(mem_prompt.reshape(b * n_mem, d), row(g_mem[0]), wck, wcv, tm=_token_tile(b * n_mem))
    tt = 512
    y_prompt = trunk_out(xp1, o_fox.reshape(b * s, w), o_ret.reshape(b * s, w), mk, mv, tt=tt,
                         keys_per_block=n_mem, mem_index=lambda i: i // (s // tt), group=None)

    pos_s = past_len + jnp.arange(ds, dtype=jnp.int32)
    xs1, qa_s, ka_s, va_s, _, _, lf_s, qr_s, kr_s, vr_s, gr_s = trunk_in(x_sample.reshape(db * ds, d), jnp.tile(pos_s, db))
    feature_major = lambda c: jnp.transpose(c, (0, 2, 3, 1)).reshape(c.shape[0], w, page)
    new_page = lambda a, width: jnp.pad(jnp.transpose(a.reshape(db, ds, width), (0, 2, 1)),
                                        ((0, 0), (0, 0), (0, page - ds)))
    o_fox_s = _fox_sample(page_table, qa_s.astype(F32).reshape(db, ds, w), new_page(ka_s, w), new_page(va_s, w),
                          new_page(lf_s, N_HEADS), feature_major(cache_fox_k[0]), feature_major(cache_fox_v[0]),
                          jnp.transpose(cache_fox_logf[0], (0, 2, 1)), pp=16)
    pad_tok = lambda a: jnp.pad(a.reshape(db, ds, w), ((0, 0), (0, RET_CHUNK - ds), (0, 0)))
    o_ret_s, st_s = _retention(pad_tok(qr_s), pad_tok(kr_s), pad_tok(vr_s), pad_tok(gr_s), row(g_ret[0]),
                               _blockdiag_states(state_ret[0]), _ret_tables(ds), n_chunks=1)
    bb = 8
    y_sample = trunk_out(xs1, bf(o_fox_s.reshape(db * ds, w)), o_ret_s[:, :ds].reshape(db * ds, w),
                         cache_mem_k[0].reshape(db * n_mem, -1), cache_mem_v[0].reshape(db * n_mem, -1),
                         tt=bb * ds, keys_per_block=bb * n_mem, mem_index=lambda i: i, group=(ds, n_mem))

    mem_heads = lambda a: a.reshape(1, b, n_mem, N_MEM_HEADS, -1)
    token_major = lambda a: jnp.transpose(a.reshape(1, b, N_HEADS, HEAD_DIM, s), (0, 1, 4, 2, 3))
    return (y_prompt.reshape(b, s, d), y_sample.reshape(db, ds, d),
            token_major(kt), token_major(vt),
            jnp.transpose(lft, (0, 2, 1))[None], _head_states(st_p)[None], mem_heads(mk), mem_heads(mv),
            ka_s.reshape(1, db, ds, N_HEADS, HEAD_DIM), va_s.reshape(1, db, ds, N_HEADS, HEAD_DIM),
            lf_s.reshape(1, db, ds, N_HEADS), _head_states(st_s)[None])
```

```python
import functools

import jax
import jax.numpy as jnp
from jax import lax
from jax.experimental import pallas as pl
from jax.experimental.pallas import tpu as pltpu

F32 = jnp.float32
BF16 = jnp.bfloat16

HEAD_DIM = 64
N_HEADS = 8
GROUP_WIDTH = N_HEADS * HEAD_DIM
N_MEM_HEADS = 4
ROPE_THETA = 10000.0
EPS = 1e-6
FFN_RES = 0.5
Q_SCALE = HEAD_DIM ** -0.5
RET_CHUNK = 128
LANES = 128
VMEM_LIMIT = 56 * 1024 * 1024


def _params(*sem):
    return pltpu.CompilerParams(dimension_semantics=sem, vmem_limit_bytes=VMEM_LIMIT)


def _resident(shape):
    nd = len(shape)
    return pl.BlockSpec(shape, lambda *_: (0,) * nd, pipeline_mode=pl.Buffered(1))


def _dot(a, b):
    return jnp.dot(a, b, preferred_element_type=F32)


def _dot_nt(a, b):
    return lax.dot_general(a, b, (((1,), (1,)), ((), ())), preferred_element_type=F32)


def _rms(x, g):
    return x * lax.rsqrt(jnp.mean(x * x, axis=-1, keepdims=True) + EPS) * g


def _silu(x):
    return x * jax.nn.sigmoid(x)


def _split3(x):
    hi = x.astype(BF16)
    r1 = x - hi.astype(F32)
    mid = r1.astype(BF16)
    lo = (r1 - mid.astype(F32)).astype(BF16)
    return hi, mid, lo


def _dot_exact01(x, w01, w_left=False):
    mm = (lambda part: _dot(w01, part)) if w_left else (lambda part: _dot(part, w01))
    hi, mid, lo = _split3(x)
    return mm(hi) + mm(mid) + mm(lo)


def _ffn_kernel(x_ref, g_ref, wg_ref, wu_ref, wd_ref, gf_ref, o_ref, *, chunks, final_norm):
    x = x_ref[...]
    h = _rms(x, g_ref[...]).astype(BF16)
    acc = jnp.zeros_like(x)
    start = 0
    for width in chunks:
        sl = slice(start, start + width)
        a = _silu(_dot(h, wg_ref[:, sl])) * _dot(h, wu_ref[:, sl])
        acc = acc + _dot(a.astype(BF16), wd_ref[sl, :])
        start += width
    y = x + FFN_RES * acc
    if final_norm:
        y = _rms(y, gf_ref[...])
    o_ref[...] = y


def _ffn(x, g, wg, wu, wd, gf, *, final_norm, tm):
    t, d = x.shape
    d_ff = wg.shape[1]
    chunks = [512] * (d_ff // 512)
    if d_ff % 512:
        chunks.append(d_ff % 512)
    row = pl.BlockSpec((tm, d), lambda i: (i, 0))
    return pl.pallas_call(
        functools.partial(_ffn_kernel, chunks=tuple(chunks), final_norm=final_norm),
        grid=(t // tm,),
        in_specs=[row, _resident((1, d)), _resident(wg.shape), _resident(wu.shape),
                  _resident(wd.shape), _resident((1, d))],
        out_specs=row,
        out_shape=jax.ShapeDtypeStruct((t, d), F32),
        compiler_params=_params("parallel"),
        name="ffn",
    )(x, g, wg, wu, wd, gf)


def _inproj_kernel(x_ref, g_ref, wfox_ref, wf_ref, bf_ref, wret_ref, cos_ref, sin_ref,
                   qa_ref, ka_ref, va_ref, kab_ref, vab_ref, lf_ref, qr_ref, kr_ref, vr_ref, gr_ref):
    h = _rms(x_ref[...], g_ref[...]).astype(BF16)
    w = GROUP_WIDTH
    qa_ref[...] = (_dot(h, wfox_ref[:, 0:w]) * Q_SCALE).astype(BF16)
    ka = _dot(h, wfox_ref[:, w:2 * w])
    ka_ref[...] = ka
    kab_ref[...] = ka.astype(BF16)
    va = _dot(h, wfox_ref[:, 2 * w:3 * w])
    va_ref[...] = va
    vab_ref[...] = va.astype(BF16)
    z = _dot(h, wf_ref[...]) + bf_ref[...]
    lf_ref[...] = (jnp.minimum(z, 0.0) - jnp.log1p(jnp.exp(-jnp.abs(z))))[:, :N_HEADS]
    cos = cos_ref[...]
    sin = sin_ref[...]
    qr_ref[...] = (_dot(h, wret_ref[:, 0:w]) * cos + _dot(h, wret_ref[:, w:2 * w]) * sin).astype(BF16)
    kr = _dot(h, wret_ref[:, 2 * w:3 * w]) * cos + _dot(h, wret_ref[:, 3 * w:4 * w]) * sin
    kr_ref[...] = (kr * Q_SCALE).astype(BF16)
    vr_ref[...] = _dot(h, wret_ref[:, 4 * w:5 * w]).astype(BF16)
    gr_ref[...] = _dot(h, wret_ref[:, 5 * w:6 * w])


def _inproj(x, g, wfox, wf, bf, wret, cos, sin, *, tm):
    t, d = x.shape
    w = GROUP_WIDTH
    n_pos = cos.shape[0] // tm
    row = lambda width: pl.BlockSpec((tm, width), lambda i: (i, 0))
    tab = pl.BlockSpec((tm, w), lambda i: (i % n_pos, 0))
    sds = lambda width, dt: jax.ShapeDtypeStruct((t, width), dt)
    return pl.pallas_call(
        _inproj_kernel,
        grid=(t // tm,),
        in_specs=[row(d), _resident((1, d)), _resident(wfox.shape), _resident(wf.shape),
                  _resident(bf.shape), _resident(wret.shape), tab, tab],
        out_specs=[row(w), row(w), row(w), row(w), row(w), row(N_HEADS), row(w), row(w), row(w), row(w)],
        out_shape=[sds(w, BF16), sds(w, F32), sds(w, F32), sds(w, BF16), sds(w, BF16), sds(N_HEADS, F32),
                   sds(w, BF16), sds(w, BF16), sds(w, BF16), sds(w, F32)],
        compiler_params=_params("parallel"),
        name="inproj",
    )(x, g, wfox, wf, bf, wret, cos, sin)


def _inproj_prompt_kernel(x_ref, g_ref, wt_ref, wk_ref, wf_ref, bf_ref, wret_ref, cos_ref, sin_ref,
                          qt_ref, kt_ref, vt_ref, kb_ref, vtb_ref, lf_ref, lft_ref, qr_ref, kr_ref, vr_ref, gr_ref):
    h = _rms(x_ref[...], g_ref[...]).astype(BF16)
    w = GROUP_WIDTH
    qt_ref[...] = (_dot_nt(wt_ref[0:w, :], h) * Q_SCALE).astype(BF16)
    kt_ref[...] = _dot_nt(wt_ref[w:2 * w, :], h)
    vt = _dot_nt(wt_ref[2 * w:3 * w, :], h)
    vt_ref[...] = vt
    tk = vtb_ref.shape[-1]
    for c in range(vtb_ref.shape[0]):
        vtb_ref[c] = vt[:, c * tk:(c + 1) * tk].astype(BF16)
    kb_ref[...] = _dot(h, wk_ref[...]).astype(BF16)
    z = _dot(h, wf_ref[...]) + bf_ref[...]
    lane = lax.broadcasted_iota(jnp.int32, z.shape, 1)
    lf = jnp.where(lane < N_HEADS, jnp.minimum(z, 0.0) - jnp.log1p(jnp.exp(-jnp.abs(z))), 0.0)
    lf_ref[...] = lf
    lft_ref[...] = lf.T[:N_HEADS, :]
    cos = cos_ref[...]
    sin = sin_ref[...]
    qr_ref[...] = (_dot(h, wret_ref[:, 0:w]) * cos + _dot(h, wret_ref[:, w:2 * w]) * sin).astype(BF16)
    kr = _dot(h, wret_ref[:, 2 * w:3 * w]) * cos + _dot(h, wret_ref[:, 3 * w:4 * w]) * sin
    kr_ref[...] = (kr * Q_SCALE).astype(BF16)
    vr_ref[...] = _dot(h, wret_ref[:, 4 * w:5 * w]).astype(BF16)
    gr_ref[...] = _dot(h, wret_ref[:, 5 * w:6 * w])


def _inproj_prompt(x, g, wt, wk, wf, bf, wret, cos, sin, *, batch, tm, tk):
    t, d = x.shape
    w = GROUP_WIDTH
    s = t // batch
    n_s = s // tm
    row = lambda width: pl.BlockSpec((tm, width), lambda i: (i, 0))
    tab = pl.BlockSpec((tm, w), lambda i: (i % n_s, 0))
    fmajor = lambda rows: pl.BlockSpec((None, rows, tm), lambda i: (i // n_s, 0, i % n_s))
    sds = jax.ShapeDtypeStruct
    return pl.pallas_call(
        _inproj_prompt_kernel,
        grid=(t // tm,),
        in_specs=[row(d), _resident((1, d)), _resident(wt.shape), _resident(wk.shape), _resident(wf.shape),
                  _resident(bf.shape), _resident(wret.shape), tab, tab],
        out_specs=[fmajor(w), fmajor(w), fmajor(w), row(w),
                   pl.BlockSpec((None, tm // tk, w, tk), lambda i: (i // n_s, i % n_s, 0, 0)),
                   row(LANES), fmajor(N_HEADS), row(w), row(w), row(w), row(w)],
        out_shape=[sds((batch, w, s), BF16), sds((batch, w, s), F32), sds((batch, w, s), F32), sds((t, w), BF16),
                   sds((batch, s // tk, w, tk), BF16), sds((t, LANES), F32), sds((batch, N_HEADS, s), F32),
                   sds((t, w), BF16), sds((t, w), BF16), sds((t, w), BF16), sds((t, w), F32)],
        compiler_params=_params("parallel"),
        name="inproj_prompt",
    )(x, g, wt, wk, wf, bf, wret, cos, sin)


BIAS_TERMS = 3


def _cumsum_kernel(lf_ref, lft_ref, fb_ref, frow_ref, carry_col, carry_row, *, tc):
    @pl.when(pl.program_id(1) == 0)
    def _():
        carry_col[...] = jnp.zeros_like(carry_col)
        carry_row[...] = jnp.zeros_like(carry_row)

    r = lax.broadcasted_iota(jnp.int32, (tc, tc), 0)
    c = lax.broadcasted_iota(jnp.int32, (tc, tc), 1)
    lower = jnp.where(c <= r, 1.0, 0.0).astype(BF16)
    fcol = _dot_exact01(lf_ref[...], lower, w_left=True) + carry_col[...]
    carry_col[...] = fcol[tc - 1:tc, :]
    rr = lax.broadcasted_iota(jnp.int32, (LANES, LANES), 0)
    cc = lax.broadcasted_iota(jnp.int32, (LANES, LANES), 1)
    fb = jnp.zeros((tc, LANES), F32)
    for j, term in enumerate(_split3(-fcol)):
        place = jnp.where((cc == BIAS_TERMS * rr + j) & (rr < N_HEADS), 1.0, 0.0).astype(BF16)
        fb = fb + _dot(term, place)
    fb_ref[...] = fb.astype(BF16)
    upper = jnp.where(r <= c, 1.0, 0.0).astype(BF16)
    frow = _dot_exact01(lft_ref[...], upper) + carry_row[...]
    frow_ref[...] = frow
    carry_row[...] = frow[:, tc - 1:tc]


def _cumsum(lf_pad, lf_t, *, tc):
    b, s, _ = lf_pad.shape
    col = pl.BlockSpec((None, tc, LANES), lambda i, j: (i, j, 0))
    rowspec = pl.BlockSpec((None, N_HEADS, tc), lambda i, j: (i, 0, j))
    return pl.pallas_call(
        functools.partial(_cumsum_kernel, tc=tc),
        grid=(b, s // tc),
        in_specs=[col, rowspec],
        out_specs=[col, rowspec],
        out_shape=[jax.ShapeDtypeStruct((b, s, LANES), BF16), jax.ShapeDtypeStruct((b, N_HEADS, s), F32)],
        scratch_shapes=[pltpu.VMEM((1, LANES), F32), pltpu.VMEM((N_HEADS, 1), F32)],
        compiler_params=_params("parallel", "arbitrary"),
        name="logf_cumsum",
    )(lf_pad, lf_t)


def _fox_prompt_kernel(qt_ref, k_ref, fb_ref, vt_ref, frow_ref, o_ref, *, tq):
    pg = pl.program_id(1)
    qi = pl.program_id(2)
    pairs = qt_ref.shape[0] // LANES
    heads = 2 * pairs
    row = lax.broadcasted_iota(jnp.int32, (LANES, tq), 0)
    qw, fq = [], []
    for p in range(pairs):
        qt = qt_ref[p * LANES:(p + 1) * LANES, :].astype(F32)
        both = []
        for h in (0, 1):
            head = 2 * (pg * pairs + p) + h
            own = (row < HEAD_DIM) if h == 0 else (row >= HEAD_DIM)
            first = BIAS_TERMS * head
            ones = jnp.where((row >= first) & (row < first + BIAS_TERMS), 1.0, 0.0)
            both.append(jnp.concatenate([jnp.where(own, qt, 0.0), ones], axis=0).astype(BF16))
            fq.append(frow_ref[pl.ds(head, 1), :])
        qw.append(jnp.concatenate(both, axis=1))
    fq = jnp.concatenate(fq, axis=1)

    def scores(kj):
        start = pl.multiple_of(kj * tq, tq)
        bias = fb_ref[pl.ds(start, tq), :]
        return jnp.concatenate(
            [_dot(jnp.concatenate([k_ref[pl.ds(start, tq), p * LANES:(p + 1) * LANES], bias], axis=1), qw[p])
             for p in range(pairs)], axis=1)

    def consume(kj, t, stats, masked):
        m_old, l_old, acc = stats
        if masked:
            key = lax.broadcasted_iota(jnp.int32, t.shape, 0)
            qry = lax.broadcasted_iota(jnp.int32, t.shape, 1) & (tq - 1)
            t = jnp.where(key <= qry, t, -jnp.inf)
        m_new = jnp.maximum(m_old, jnp.max(t, axis=0, keepdims=True) + fq)
        p = jnp.exp(t + (fq - m_new))
        alpha = jnp.exp(m_old - m_new)
        pb = p.astype(BF16)
        pv = jnp.concatenate([_dot(vt_ref[kj, h * HEAD_DIM:(h + 1) * HEAD_DIM, :], pb[:, h * tq:(h + 1) * tq])
                              for h in range(heads)], axis=1)
        return m_new, alpha * l_old + jnp.sum(p, axis=0, keepdims=True), alpha * acc + pv

    def body(kj, carry):
        t_cur, stats = carry
        t_next = scores(kj + 1)
        return t_next, consume(kj, t_cur, stats, False)

    init = (jnp.full((1, heads * tq), -1e30, F32), jnp.zeros((1, heads * tq), F32),
            jnp.zeros((HEAD_DIM, heads * tq), F32))
    t_last, stats = lax.fori_loop(0, qi, body, (scores(0), init))
    _, l_fin, acc = consume(qi, t_last, stats, True)
    out = acc / l_fin
    o_ref[...] = jnp.concatenate([out[:, h * tq:(h + 1) * tq] for h in range(heads)], axis=0).T.astype(o_ref.dtype)


def _fox_prompt(qt, k, fb, vtb, frow, *, tq, pairs):
    b, w, s = qt.shape
    wb = pairs * LANES
    nk = s // tq
    assert tq & (tq - 1) == 0
    return pl.pallas_call(
        functools.partial(_fox_prompt_kernel, tq=tq),
        grid=(b, w // wb, nk),
        in_specs=[pl.BlockSpec((None, wb, tq), lambda i, p, j: (i, p, j)),
                  pl.BlockSpec((None, s, wb), lambda i, p, j: (i, 0, p)),
                  pl.BlockSpec((None, s, LANES), lambda i, p, j: (i, 0, 0)),
                  pl.BlockSpec((None, nk, wb, tq), lambda i, p, j: (i, 0, p, 0)),
                  pl.BlockSpec((None, N_HEADS, tq), lambda i, p, j: (i, 0, j))],
        out_specs=pl.BlockSpec((None, tq, wb), lambda i, p, j: (i, j, p)),
        out_shape=jax.ShapeDtypeStruct((b, s, w), BF16),
        compiler_params=_params("parallel", "parallel", "arbitrary"),
        name="fox_prompt",
    )(qt, k, fb, vtb, frow)


def _ret_kernel(q_ref, k_ref, v_ref, gate_ref, gret_ref, st0_ref, dec_ref, cross_ref, kdec_ref, gam_ref,
                o_ref, st_ref, state, *, n_chunks):
    c = RET_CHUNK
    j = pl.program_id(1)

    @pl.when(j == 0)
    def _():
        state[...] = st0_ref[...]

    lane = lax.broadcasted_iota(jnp.int32, (c, LANES), 1)
    lo = lane < HEAD_DIM
    r = lax.broadcasted_iota(jnp.int32, (LANES, LANES), 0)
    cc = lax.broadcasted_iota(jnp.int32, (LANES, LANES), 1)
    same_head = (r < HEAD_DIM) == (cc < HEAD_DIM)
    mean_w = jnp.where(same_head, 1.0 / HEAD_DIM, 0.0).astype(BF16)
    for ci in range(n_chunks):
        rows = slice(ci * c, (ci + 1) * c)
        outs = []
        for g in range(GROUP_WIDTH // LANES):
            ls = slice(g * LANES, (g + 1) * LANES)
            qg = q_ref[rows, ls]
            kg = k_ref[rows, ls]
            vg = v_ref[rows, ls]
            zero = jnp.zeros_like(qg)
            s0 = _dot_nt(jnp.where(lo, qg, zero), kg) * dec_ref[2 * g]
            s1 = _dot_nt(jnp.where(lo, zero, qg), kg) * dec_ref[2 * g + 1]
            o_inner = jnp.where(lo, _dot(s0.astype(BF16), vg), _dot(s1.astype(BF16), vg))
            st = state[g]
            o = o_inner + _dot(qg, st.astype(BF16)) * cross_ref[:, ls]
            kd = kg.astype(F32) * kdec_ref[:, ls]
            upd = _dot(kd.T.astype(BF16), vg)
            state[g] = gam_ref[g] * st + jnp.where(same_head, upd, 0.0)
            o2 = o * o
            hi = o2.astype(BF16)
            ms = _dot(hi, mean_w) + _dot((o2 - hi.astype(F32)).astype(BF16), mean_w)
            of = o * lax.rsqrt(ms + EPS)
            outs.append(((of * gret_ref[:, ls]) * _silu(gate_ref[rows, ls])).astype(BF16))
        o_ref[rows, :] = jnp.concatenate(outs, axis=1)

    @pl.when(j == pl.num_programs(1) - 1)
    def _():
        st_ref[...] = state[...]


def _retention(q, k, v, gate, gret, st0, tables, *, n_chunks):
    b, s, w = q.shape
    tr = n_chunks * RET_CHUNK
    groups = w // LANES
    dec, cross, kdec, gam = tables
    tok = pl.BlockSpec((None, tr, w), lambda i, j: (i, j, 0))
    stspec = pl.BlockSpec((None, groups, LANES, LANES), lambda i, j: (i, 0, 0, 0))
    return pl.pallas_call(
        functools.partial(_ret_kernel, n_chunks=n_chunks),
        grid=(b, s // tr),
        in_specs=[tok, tok, tok, tok, _resident((1, w)), stspec, _resident(dec.shape), _resident(cross.shape),
                  _resident(kdec.shape), _resident(gam.shape)],
        out_specs=[tok, stspec],
        out_shape=[jax.ShapeDtypeStruct((b, s, w), BF16), jax.ShapeDtypeStruct((b, groups, LANES, LANES), F32)],
        scratch_shapes=[pltpu.VMEM((groups, LANES, LANES), F32)],
        compiler_params=_params("parallel", "arbitrary"),
        name="retention",
    )(q, k, v, gate, gret, st0, dec, cross, kdec, gam)


def _ret_tables(chunk_len):
    c = RET_CHUNK
    log_gamma = jnp.log(1.0 - 2.0 ** (-5.0 - jnp.arange(N_HEADS, dtype=F32)))
    idx = jnp.arange(c, dtype=F32)
    diff = idx[:, None] - idx[None, :]
    dec = jnp.where(diff >= 0, jnp.exp(jnp.maximum(diff, 0.0)[None] * log_gamma[:, None, None]), 0.0)
    cross = jnp.exp((idx + 1.0)[:, None] * log_gamma[None, :])
    kdec = jnp.exp((chunk_len - 1.0 - idx)[:, None] * log_gamma[None, :])
    sdec = jnp.exp(chunk_len * log_gamma)
    head_of_lane = jnp.arange(GROUP_WIDTH) // HEAD_DIM
    row_head = head_of_lane.reshape(GROUP_WIDTH // LANES, LANES)
    same = (jnp.arange(LANES)[:, None] // HEAD_DIM) == (jnp.arange(LANES)[None, :] // HEAD_DIM)
    gam = jnp.where(same[None], sdec[row_head][:, :, None], 0.0)
    return dec, cross[:, head_of_lane], kdec[:, head_of_lane], gam


def _ret_sample_kernel(q_ref, k_ref, v_ref, gate_ref, gret_ref, tab_ref, st_ref, o_ref, sto_ref):
    n_new = q_ref.shape[0]
    const = lambda i: tab_ref[i:i + 1, :]
    q = [q_ref[l] for l in range(n_new)]
    k = [k_ref[l] for l in range(n_new)]
    v = [v_ref[l] for l in range(n_new)]
    inner = []
    for l in range(n_new):
        o = jnp.zeros_like(v[0])
        for m in range(l + 1):
            o = o + (jnp.sum(q[l] * k[m], axis=0, keepdims=True) * const(l - m)) * v[m]
        inner.append(o)
    state_decay = const(3 * n_new)

    def per_key_dim(d, cross):
        s_d = st_ref[d]
        new = state_decay * s_d
        out = []
        for l in range(n_new):
            out.append(cross[l] + q_ref[l, pl.ds(d, 1), :] * s_d)
            new = new + (k_ref[l, pl.ds(d, 1), :] * const(2 * n_new + l)) * v[l]
        sto_ref[d] = new
        return tuple(out)

    cross = lax.fori_loop(0, st_ref.shape[0], per_key_dim, tuple(jnp.zeros_like(v[0]) for _ in range(n_new)))
    for l in range(n_new):
        o = inner[l] + cross[l] * const(n_new + l)
        of = o * lax.rsqrt(jnp.mean(o * o, axis=0, keepdims=True) + EPS)
        o_ref[l] = ((of * gret_ref[...]) * _silu(gate_ref[l])).astype(o_ref.dtype)


def _ret_sample(q, k, v, gate, gret, tab, st):
    n_new, nh, hd, nb = q.shape
    tok = pl.BlockSpec((n_new, None, hd, nb), lambda h: (0, h, 0, 0))
    stspec = pl.BlockSpec((None, hd, hd, nb), lambda h: (h, 0, 0, 0))
    return pl.pallas_call(
        _ret_sample_kernel,
        grid=(nh,),
        in_specs=[tok, tok, tok, tok, pl.BlockSpec((None, hd, 1), lambda h: (h, 0, 0)),
                  pl.BlockSpec((None,) + tab.shape[1:], lambda h: (h, 0, 0)), stspec],
        out_specs=[tok, stspec],
        out_shape=[jax.ShapeDtypeStruct(q.shape, BF16), jax.ShapeDtypeStruct(st.shape, F32)],
        compiler_params=_params("parallel"),
        name="retention_sample",
    )(q, k, v, gate, gret, tab, st)


def _ret_sample_table(n_new, lanes):
    log_gamma = jnp.log(1.0 - 2.0 ** (-5.0 - jnp.arange(N_HEADS, dtype=F32)))[:, None]
    idx = jnp.arange(n_new, dtype=F32)[None, :]
    cols = jnp.concatenate([jnp.exp(idx * log_gamma), jnp.exp((idx + 1.0) * log_gamma),
                            jnp.exp((n_new - 1.0 - idx) * log_gamma), jnp.exp(n_new * log_gamma)], axis=1)
    rows = -(-cols.shape[1] // 8) * 8
    cols = jnp.pad(cols, ((0, 0), (0, rows - cols.shape[1])))
    return jnp.broadcast_to(cols[:, :, None], (N_HEADS, rows, lanes))


def _head_states(bd):
    b = bd.shape[0]
    a = bd[:, :, :HEAD_DIM, :HEAD_DIM]
    d = bd[:, :, HEAD_DIM:, HEAD_DIM:]
    return jnp.stack([a, d], axis=2).reshape(b, N_HEADS, HEAD_DIM, HEAD_DIM)


def _fox_sample_kernel(pt_ref, q_ref, kn_ref, vn_ref, lfn_ref, *rest, pp):
    k_refs = rest[:pp]
    v_refs = rest[pp:2 * pp]
    lf_refs = rest[2 * pp:3 * pp]
    o_ref = rest[3 * pp]
    qbd_s, m_s, l_s, acc_s, carry_s, fq_s = rest[3 * pp + 1:]
    del pt_ref
    j = pl.program_id(1)
    n_new, w = q_ref.shape
    rows = n_new * N_HEADS
    page = k_refs[0].shape[1]
    sub = lax.broadcasted_iota(jnp.int32, (N_HEADS, w), 0)
    own = (lax.broadcasted_iota(jnp.int32, (N_HEADS, w), 1) // HEAD_DIM) == sub
    tr = lax.broadcasted_iota(jnp.int32, (page, page), 0)
    tc = lax.broadcasted_iota(jnp.int32, (page, page), 1)
    tile_rows = lambda x: jnp.concatenate([x] * n_new, axis=0)

    @pl.when(j == 0)
    def _():
        q = q_ref[...]
        qbd = jnp.concatenate([jnp.where(own, jnp.broadcast_to(q[t:t + 1, :], (N_HEADS, w)), 0.0)
                               for t in range(n_new)], axis=0).astype(BF16)
        qbd_s[...] = qbd
        incl = jnp.where(tr <= tc, 1.0, 0.0).astype(BF16)
        cum = _dot_exact01(lfn_ref[...], incl)
        fq = jnp.concatenate([cum[:, t:t + 1] for t in range(n_new)], axis=0)
        fq_s[...] = fq
        s = _dot(qbd, kn_ref[...].astype(BF16))
        t_of_row = lax.broadcasted_iota(jnp.int32, (rows, page), 0) // N_HEADS
        m_idx = lax.broadcasted_iota(jnp.int32, (rows, page), 1)
        logits = jnp.where(m_idx <= t_of_row, s + fq - tile_rows(cum), -jnp.inf)
        m0 = jnp.max(logits, axis=1, keepdims=True)
        p = jnp.exp(logits - m0)
        m_s[...] = m0
        l_s[...] = jnp.sum(p, axis=1, keepdims=True)
        acc_s[...] = _dot_nt(p.astype(BF16), vn_ref[...].astype(BF16))
        carry_s[...] = jnp.zeros_like(carry_s)

    lfs = [lf_refs[i][...] for i in range(pp)]
    strict = jnp.where(tr > tc, 1.0, 0.0).astype(BF16)
    suffix = _dot_exact01(jnp.concatenate(lfs, axis=0), strict)
    carry = carry_s[...]
    biases = []
    for i in range(pp):
        d_i = suffix[i * N_HEADS:(i + 1) * N_HEADS]
        biases.append(tile_rows(d_i + carry))
        carry = carry + d_i[:, 0:1] + lfs[i][:, 0:1]
    carry_s[...] = carry
    kcat = jnp.concatenate([k_refs[i][...].astype(BF16) for i in range(pp)], axis=1)
    t = _dot(qbd_s[...], kcat) + jnp.concatenate(biases, axis=1)
    fq = fq_s[...]
    m_old = m_s[...]
    m_new = jnp.maximum(m_old, jnp.max(t, axis=1, keepdims=True) + fq)
    p = jnp.exp(t + (fq - m_new))
    alpha = jnp.exp(m_old - m_new)
    vcat = jnp.concatenate([v_refs[i][...].astype(BF16) for i in range(pp)], axis=1)
    l_new = alpha * l_s[...] + jnp.sum(p, axis=1, keepdims=True)
    acc = alpha * acc_s[...] + _dot_nt(p.astype(BF16), vcat)
    m_s[...] = m_new
    l_s[...] = l_new
    acc_s[...] = acc

    @pl.when(j == pl.num_programs(1) - 1)
    def _():
        out = acc / l_new
        o_ref[...] = jnp.concatenate(
            [jnp.sum(jnp.where(own, out[t * N_HEADS:(t + 1) * N_HEADS], 0.0), axis=0, keepdims=True)
             for t in range(n_new)], axis=0)


def _fox_sample(page_table, q, k_new_t, v_new_t, lf_new_t, cache_k_t, cache_v_t, cache_lf_t, *, pp):
    b, n_new, w = q.shape
    n_pages = page_table.shape[1]
    page = cache_k_t.shape[2]
    rows = n_new * N_HEADS
    per_row = lambda shape: pl.BlockSpec((None,) + shape, lambda i, j, pt: (i, 0, 0))

    def paged(shape, slot):
        return pl.BlockSpec((None,) + shape, lambda i, j, pt: (pt[i, n_pages - 1 - (j * pp + slot)], 0, 0))

    kv = [paged((w, page), s) for s in range(pp)]
    lf = [paged((N_HEADS, page), s) for s in range(pp)]
    grid_spec = pltpu.PrefetchScalarGridSpec(
        num_scalar_prefetch=1,
        grid=(b, n_pages // pp),
        in_specs=[per_row((n_new, w)), per_row((w, page)), per_row((w, page)), per_row((N_HEADS, page))] + kv + kv + lf,
        out_specs=per_row((n_new, w)),
        scratch_shapes=[pltpu.VMEM((rows, w), BF16), pltpu.VMEM((rows, 1), F32), pltpu.VMEM((rows, 1), F32),
                        pltpu.VMEM((rows, w), F32), pltpu.VMEM((N_HEADS, 1), F32), pltpu.VMEM((rows, 1), F32)],
    )
    return pl.pallas_call(
        functools.partial(_fox_sample_kernel, pp=pp),
        grid_spec=grid_spec,
        out_shape=jax.ShapeDtypeStruct(q.shape, F32),
        compiler_params=_params("parallel", "arbitrary"),
        name="fox_sample_paged",
    )(page_table, q, k_new_t, v_new_t, lf_new_t, *([cache_k_t] * pp), *([cache_v_t] * pp), *([cache_lf_t] * pp))


def _outproj_kernel(x_ref, of_ref, or_ref, wo_ref, g_ref, wq_ref, x2_ref, q_ref):
    w = GROUP_WIDTH
    x2 = x_ref[...] + _dot(of_ref[...], wo_ref[0:w, :]) + _dot(or_ref[...], wo_ref[w:2 * w, :])
    x2_ref[...] = x2
    q_ref[...] = _dot(_rms(x2, g_ref[...]).astype(BF16), wq_ref[...]) * Q_SCALE


def _outproj(x, o_fox, o_ret, w_out, g_cross, w_cq, *, tm):
    t, d = x.shape
    w = GROUP_WIDTH
    mw = w_cq.shape[1]
    row = lambda width: pl.BlockSpec((tm, width), lambda i: (i, 0))
    return pl.pallas_call(
        _outproj_kernel,
        grid=(t // tm,),
        in_specs=[row(d), row(w), row(w), _resident(w_out.shape), _resident((1, d)), _resident(w_cq.shape)],
        out_specs=[row(d), row(mw)],
        out_shape=[jax.ShapeDtypeStruct((t, d), F32), jax.ShapeDtypeStruct((t, mw), F32)],
        compiler_params=_params("parallel"),
        name="outproj",
    )(x, o_fox, o_ret, w_out, g_cross, w_cq)


def _xattn_kernel(x_ref, q_ref, mk_ref, mv_ref, wo_ref, o_ref, *, group):
    tt, mw = q_ref.shape
    n_keys = mk_ref.shape[0]
    q = q_ref[...]
    lane = lax.broadcasted_iota(jnp.int32, (tt, mw), 1)
    qs = jnp.concatenate([jnp.where(lane // HEAD_DIM == h, q, 0.0) for h in range(N_MEM_HEADS)], axis=0)
    s = _dot_nt(qs.astype(BF16), mk_ref[...].astype(BF16))
    if group is not None:
        tok_b = (lax.broadcasted_iota(jnp.int32, s.shape, 0) % tt) // group[0]
        key_b = lax.broadcasted_iota(jnp.int32, s.shape, 1) // group[1]
        s = jnp.where(tok_b == key_b, s, -jnp.inf)
    e = jnp.exp(s - jnp.max(s, axis=1, keepdims=True))
    p = (e / jnp.sum(e, axis=1, keepdims=True)).astype(BF16)
    pv = _dot(p, mv_ref[...].astype(BF16))
    o = jnp.zeros((tt, mw), F32)
    for h in range(N_MEM_HEADS):
        o = o + jnp.where(lane // HEAD_DIM == h, pv[h * tt:(h + 1) * tt], 0.0)
    o_ref[...] = x_ref[...] + _dot(o.astype(BF16), wo_ref[...])
    del n_keys


def _xattn(x, q, mk, mv, w_co, *, tt, keys_per_block, mem_index, group):
    t, d = x.shape
    mw = q.shape[1]
    row = lambda width: pl.BlockSpec((tt, width), lambda i: (i, 0))
    mem = pl.BlockSpec((keys_per_block, mw), lambda i: (mem_index(i), 0))
    return pl.pallas_call(
        functools.partial(_xattn_kernel, group=group),
        grid=(t // tt,),
        in_specs=[row(d), row(mw), mem, mem, _resident(w_co.shape)],
        out_specs=row(d),
        out_shape=jax.ShapeDtypeStruct((t, d), F32),
        compiler_params=_params("parallel"),
        name="cross_attention",
    )(x, q, mk, mv, w_co)


def _memkv_kernel(m_ref, g_ref, wk_ref, wv_ref, k_ref, v_ref):
    h = _rms(m_ref[...], g_ref[...]).astype(BF16)
    k_ref[...] = _dot(h, wk_ref[...])
    v_ref[...] = _dot(h, wv_ref[...])


def _memkv(mem, g, wk, wv, *, tm):
    t, d = mem.shape
    mw = wk.shape[1]
    row = lambda width: pl.BlockSpec((tm, width), lambda i: (i, 0))
    return pl.pallas_call(
        _memkv_kernel,
        grid=(t // tm,),
        in_specs=[row(d), _resident((1, d)), _resident(wk.shape), _resident(wv.shape)],
        out_specs=[row(mw), row(mw)],
        out_shape=[jax.ShapeDtypeStruct((t, mw), F32)] * 2,
        compiler_params=_params("parallel"),
        name="memory_kv",
    )(mem, g, wk, wv)


def _rope_tables(pos):
    half = HEAD_DIM // 2
    inv = ROPE_THETA ** (-jnp.arange(half, dtype=F32) / half)
    ang = pos.astype(F32)[:, None] * inv[None, :]
    cos = jnp.cos(ang)
    sin = jnp.sin(ang)
    return (jnp.tile(jnp.concatenate([cos, cos], axis=-1), (1, N_HEADS)),
            jnp.tile(jnp.concatenate([-sin, sin], axis=-1), (1, N_HEADS)))


def _token_tile(t):
    return 512 if t % 512 == 0 else t


def kernel(x_prompt, x_sample, mem_prompt, cache_fox_k, cache_fox_v, cache_fox_logf, state_ret, cache_mem_k, cache_mem_v, page_table, g_ffn1, w1_gate, w1_up, w1_down, g_mix, w_in, b_f, g_ret, w_out, g_cross, g_mem, w_cq, w_ck, w_cv, w_co, g_ffn2, w2_gate, w2_up, w2_down, g_final):
    depth = w_in.shape[0]
    assert depth == 1, "single-layer trunk"
    b, s, d = x_prompt.shape
    db, ds, _ = x_sample.shape
    n_mem = mem_prompt.shape[1]
    page = cache_fox_k.shape[2]
    past_len = page_table.shape[1] * page
    w = GROUP_WIDTH
    assert s % RET_CHUNK == 0 and ds <= RET_CHUNK

    row = lambda a: a.reshape(1, -1)
    bf = lambda a: a.astype(BF16)
    wi = w_in[0]
    fox_end = 3 * w
    wfox = bf(wi[:, :fox_end])
    wf = bf(jnp.pad(wi[:, fox_end:fox_end + N_HEADS], ((0, 0), (0, LANES - N_HEADS))))
    bfp = jnp.pad(row(b_f[0]), ((0, 0), (0, LANES - N_HEADS)))
    ret0 = fox_end + N_HEADS
    wq_r, wk_r, wv_r, wg_r = (wi[:, ret0 + i * w:ret0 + (i + 1) * w] for i in range(4))
    lane = jnp.arange(w)
    swap = jnp.where(lane % HEAD_DIM < HEAD_DIM // 2, lane + HEAD_DIM // 2, lane - HEAD_DIM // 2)
    wret = bf(jnp.concatenate([wq_r, wq_r[:, swap], wk_r, wk_r[:, swap], wv_r, wg_r], axis=1))
    w1 = (bf(w1_gate[0]), bf(w1_up[0]), bf(w1_down[0]))
    w2 = (bf(w2_gate[0]), bf(w2_up[0]), bf(w2_down[0]))
    wo, wcq, wck, wcv, wco = bf(w_out[0]), bf(w_cq[0]), bf(w_ck[0]), bf(w_cv[0]), bf(w_co[0])
    gf = row(g_final)

    def trunk_in(x, pos):
        tm = _token_tile(x.shape[0])
        x1 = _ffn(x, row(g_ffn1[0]), *w1, gf, final_norm=False, tm=tm)
        cos, sin = _rope_tables(pos)
        return (x1,) + tuple(_inproj(x1, row(g_mix[0]), wfox, wf, bfp, wret, cos, sin, tm=tm))

    def trunk_out(x1, o_fox, o_ret, mk, mv, *, tt, keys_per_block, mem_index, group):
        tm = _token_tile(x1.shape[0])
        x2, qc = _outproj(x1, o_fox, o_ret, wo, row(g_cross[0]), wcq, tm=tm)
        x3 = _xattn(x2, qc, mk, mv, wco, tt=tt, keys_per_block=keys_per_block, mem_index=mem_index, group=group)
        return _ffn(x3, row(g_ffn2[0]), *w2, gf, final_norm=True, tm=tm)

    tq = 256
    tm_p = _token_tile(b * s)
    xp1 = _ffn(x_prompt.reshape(b * s, d), row(g_ffn1[0]), *w1, gf, final_norm=False, tm=tm_p)
    cos_p, sin_p = _rope_tables(jnp.arange(s, dtype=jnp.int32))
    wi_t = wi.T
    qt, kt, vt, kab, vtb, lfp, lft, qr, kr, vr, gr = _inproj_prompt(
        xp1, row(g_mix[0]), bf(wi_t[:fox_end]), bf(wi[:, w:2 * w]), wf, bfp, wret, cos_p, sin_p,
        batch=b, tm=tm_p, tk=tq)
    fb, frow = _cumsum(lfp.reshape(b, s, LANES), lft, tc=512)
    o_fox = _fox_prompt(qt, kab.reshape(b, s, w), fb, vtb, frow, tq=tq, pairs=2)
    st_zero = jnp.zeros((b, w // LANES, LANES, LANES), F32)
    o_ret, st_p = _retention(qr.reshape(b, s, w), kr.reshape(b, s, w), vr.reshape(b, s, w), gr.reshape(b, s, w),
                             row(g_ret[0]), st_zero, _ret_tables(RET_CHUNK), n_chunks=4)
    mk, mv = _memkv(mem_prompt.reshape(b * n_mem, d), row(g_mem[0]), wck, wcv, tm=_token_tile(b * n_mem))
    tt = 512
    y_prompt = trunk_out(xp1, o_fox.reshape(b * s, w), o_ret.reshape(b * s, w), mk, mv, tt=tt,
                         keys_per_block=n_mem, mem_index=lambda i: i // (s // tt), group=None)

    pos_s = past_len + jnp.arange(ds, dtype=jnp.int32)
    xs1, qa_s, ka_s, va_s, _, _, lf_s, qr_s, kr_s, vr_s, gr_s = trunk_in(x_sample.reshape(db * ds, d), jnp.tile(pos_s, db))
    feature_major = lambda c: jnp.transpose(c, (0, 2, 3, 1)).reshape(c.shape[0], w, page)
    new_page = lambda a, width: jnp.pad(jnp.transpose(a.reshape(db, ds, width), (0, 2, 1)),
                                        ((0, 0), (0, 0), (0, page - ds)))
    o_fox_s = _fox_sample(page_table, qa_s.astype(F32).reshape(db, ds, w), new_page(ka_s, w), new_page(va_s, w),
                          new_page(lf_s, N_HEADS), feature_major(cache_fox_k[0]), feature_major(cache_fox_v[0]),
                          jnp.transpose(cache_fox_logf[0], (0, 2, 1)), pp=min(32, page_table.shape[1]))
    batch_minor = lambda a: jnp.transpose(a.astype(F32).reshape(db, ds, N_HEADS, HEAD_DIM), (1, 2, 3, 0))
    o_ret_t, st_s_t = _ret_sample(batch_minor(qr_s), batch_minor(kr_s), batch_minor(vr_s), batch_minor(gr_s),
                                  g_ret[0].reshape(N_HEADS, HEAD_DIM, 1), _ret_sample_table(ds, db),
                                  jnp.transpose(state_ret[0], (1, 2, 3, 0)))
    o_ret_s = jnp.transpose(o_ret_t, (3, 0, 1, 2)).reshape(db * ds, w)
    bb = 8
    y_sample = trunk_out(xs1, bf(o_fox_s.reshape(db * ds, w)), o_ret_s,
                         cache_mem_k[0].reshape(db * n_mem, -1), cache_mem_v[0].reshape(db * n_mem, -1),
                         tt=bb * ds, keys_per_block=bb * n_mem, mem_index=lambda i: i, group=(ds, n_mem))

    mem_heads = lambda a: a.reshape(1, b, n_mem, N_MEM_HEADS, -1)
    token_major = lambda a: jnp.transpose(a.reshape(1, b, N_HEADS, HEAD_DIM, s), (0, 1, 4, 2, 3))
    return (y_prompt.reshape(b, s, d), y_sample.reshape(db, ds, d),
            token_major(kt), token_major(vt),
            jnp.transpose(lft, (0, 2, 1))[None], _head_states(st_p)[None], mem_heads(mk), mem_heads(mv),
            ka_s.reshape(1, db, ds, N_HEADS, HEAD_DIM), va_s.reshape(1, db, ds, N_HEADS, HEAD_DIM),
            lf_s.reshape(1, db, ds, N_HEADS), jnp.transpose(st_s_t, (3, 0, 1, 2))[None])
```

```python
import functools

import jax
import jax.numpy as jnp
from jax import lax
from jax.experimental import pallas as pl
from jax.experimental.pallas import tpu as pltpu

F32 = jnp.float32
BF16 = jnp.bfloat16

HEAD_DIM = 64
N_HEADS = 8
GROUP_WIDTH = N_HEADS * HEAD_DIM
N_MEM_HEADS = 4
ROPE_THETA = 10000.0
EPS = 1e-6
FFN_RES = 0.5
Q_SCALE = HEAD_DIM ** -0.5
RET_CHUNK = 128
LANES = 128
VMEM_LIMIT = 56 * 1024 * 1024


def _params(*sem):
    return pltpu.CompilerParams(dimension_semantics=sem, vmem_limit_bytes=VMEM_LIMIT)


def _resident(shape):
    nd = len(shape)
    return pl.BlockSpec(shape, lambda *_: (0,) * nd, pipeline_mode=pl.Buffered(1))


def _dot(a, b):
    return jnp.dot(a, b, preferred_element_type=F32)


def _dot_nt(a, b):
    return lax.dot_general(a, b, (((1,), (1,)), ((), ())), preferred_element_type=F32)


def _rms(x, g):
    return x * lax.rsqrt(jnp.mean(x * x, axis=-1, keepdims=True) + EPS) * g


def _silu(x):
    return x * jax.nn.sigmoid(x)


def _split3(x):
    hi = x.astype(BF16)
    r1 = x - hi.astype(F32)
    mid = r1.astype(BF16)
    lo = (r1 - mid.astype(F32)).astype(BF16)
    return hi, mid, lo


def _dot_exact01(x, w01, w_left=False):
    mm = (lambda part: _dot(w01, part)) if w_left else (lambda part: _dot(part, w01))
    hi, mid, lo = _split3(x)
    return mm(hi) + mm(mid) + mm(lo)


def _ffn_kernel(x_ref, g_ref, wg_ref, wu_ref, wd_ref, gf_ref, o_ref, *, chunks, final_norm):
    x = x_ref[...]
    h = _rms(x, g_ref[...]).astype(BF16)
    acc = jnp.zeros_like(x)
    start = 0
    for width in chunks:
        sl = slice(start, start + width)
        a = _silu(_dot(h, wg_ref[:, sl])) * _dot(h, wu_ref[:, sl])
        acc = acc + _dot(a.astype(BF16), wd_ref[sl, :])
        start += width
    y = x + FFN_RES * acc
    if final_norm:
        y = _rms(y, gf_ref[...])
    o_ref[...] = y


def _ffn(x, g, wg, wu, wd, gf, *, final_norm, tm):
    t, d = x.shape
    d_ff = wg.shape[1]
    chunks = [512] * (d_ff // 512)
    if d_ff % 512:
        chunks.append(d_ff % 512)
    row = pl.BlockSpec((tm, d), lambda i: (i, 0))
    return pl.pallas_call(
        functools.partial(_ffn_kernel, chunks=tuple(chunks), final_norm=final_norm),
        grid=(t // tm,),
        in_specs=[row, _resident((1, d)), _resident(wg.shape), _resident(wu.shape),
                  _resident(wd.shape), _resident((1, d))],
        out_specs=row,
        out_shape=jax.ShapeDtypeStruct((t, d), F32),
        compiler_params=_params("parallel"),
        name="ffn",
    )(x, g, wg, wu, wd, gf)


def _inproj_kernel(x_ref, g_ref, wfox_ref, wf_ref, bf_ref, wret_ref, cos_ref, sin_ref,
                   qa_ref, ka_ref, va_ref, kab_ref, vab_ref, lf_ref, qr_ref, kr_ref, vr_ref, gr_ref):
    h = _rms(x_ref[...], g_ref[...]).astype(BF16)
    w = GROUP_WIDTH
    qa_ref[...] = (_dot(h, wfox_ref[:, 0:w]) * Q_SCALE).astype(BF16)
    ka = _dot(h, wfox_ref[:, w:2 * w])
    ka_ref[...] = ka
    kab_ref[...] = ka.astype(BF16)
    va = _dot(h, wfox_ref[:, 2 * w:3 * w])
    va_ref[...] = va
    vab_ref[...] = va.astype(BF16)
    z = _dot(h, wf_ref[...]) + bf_ref[...]
    lf_ref[...] = (jnp.minimum(z, 0.0) - jnp.log1p(jnp.exp(-jnp.abs(z))))[:, :N_HEADS]
    cos = cos_ref[...]
    sin = sin_ref[...]
    qr_ref[...] = (_dot(h, wret_ref[:, 0:w]) * cos + _dot(h, wret_ref[:, w:2 * w]) * sin).astype(BF16)
    kr = _dot(h, wret_ref[:, 2 * w:3 * w]) * cos + _dot(h, wret_ref[:, 3 * w:4 * w]) * sin
    kr_ref[...] = (kr * Q_SCALE).astype(BF16)
    vr_ref[...] = _dot(h, wret_ref[:, 4 * w:5 * w]).astype(BF16)
    gr_ref[...] = _dot(h, wret_ref[:, 5 * w:6 * w])


def _inproj(x, g, wfox, wf, bf, wret, cos, sin, *, tm):
    t, d = x.shape
    w = GROUP_WIDTH
    n_pos = cos.shape[0] // tm
    row = lambda width: pl.BlockSpec((tm, width), lambda i: (i, 0))
    tab = pl.BlockSpec((tm, w), lambda i: (i % n_pos, 0))
    sds = lambda width, dt: jax.ShapeDtypeStruct((t, width), dt)
    return pl.pallas_call(
        _inproj_kernel,
        grid=(t // tm,),
        in_specs=[row(d), _resident((1, d)), _resident(wfox.shape), _resident(wf.shape),
                  _resident(bf.shape), _resident(wret.shape), tab, tab],
        out_specs=[row(w), row(w), row(w), row(w), row(w), row(N_HEADS), row(w), row(w), row(w), row(w)],
        out_shape=[sds(w, BF16), sds(w, F32), sds(w, F32), sds(w, BF16), sds(w, BF16), sds(N_HEADS, F32),
                   sds(w, BF16), sds(w, BF16), sds(w, BF16), sds(w, F32)],
        compiler_params=_params("parallel"),
        name="inproj",
    )(x, g, wfox, wf, bf, wret, cos, sin)


def _inproj_prompt_kernel(x_ref, g_ref, wt_ref, wk_ref, wf_ref, bf_ref, wret_ref, cos_ref, sin_ref,
                          qt_ref, kt_ref, vt_ref, kb_ref, vtb_ref, lf_ref, lft_ref, qr_ref, kr_ref, vr_ref, gr_ref):
    h = _rms(x_ref[...], g_ref[...]).astype(BF16)
    w = GROUP_WIDTH
    qt_ref[...] = (_dot_nt(wt_ref[0:w, :], h) * Q_SCALE).astype(BF16)
    kt_ref[...] = _dot_nt(wt_ref[w:2 * w, :], h)
    vt = _dot_nt(wt_ref[2 * w:3 * w, :], h)
    vt_ref[...] = vt
    tk = vtb_ref.shape[-1]
    for c in range(vtb_ref.shape[0]):
        vtb_ref[c] = vt[:, c * tk:(c + 1) * tk].astype(BF16)
    kb_ref[...] = _dot(h, wk_ref[...]).astype(BF16)
    z = _dot(h, wf_ref[...]) + bf_ref[...]
    lane = lax.broadcasted_iota(jnp.int32, z.shape, 1)
    lf = jnp.where(lane < N_HEADS, jnp.minimum(z, 0.0) - jnp.log1p(jnp.exp(-jnp.abs(z))), 0.0)
    lf_ref[...] = lf
    lft_ref[...] = lf.T[:N_HEADS, :]
    cos = cos_ref[...]
    sin = sin_ref[...]
    qr_ref[...] = (_dot(h, wret_ref[:, 0:w]) * cos + _dot(h, wret_ref[:, w:2 * w]) * sin).astype(BF16)
    kr = _dot(h, wret_ref[:, 2 * w:3 * w]) * cos + _dot(h, wret_ref[:, 3 * w:4 * w]) * sin
    kr_ref[...] = (kr * Q_SCALE).astype(BF16)
    vr_ref[...] = _dot(h, wret_ref[:, 4 * w:5 * w]).astype(BF16)
    gr_ref[...] = _dot(h, wret_ref[:, 5 * w:6 * w])


def _inproj_prompt(x, g, wt, wk, wf, bf, wret, cos, sin, *, batch, tm, tk):
    t, d = x.shape
    w = GROUP_WIDTH
    s = t // batch
    n_s = s // tm
    row = lambda width: pl.BlockSpec((tm, width), lambda i: (i, 0))
    tab = pl.BlockSpec((tm, w), lambda i: (i % n_s, 0))
    fmajor = lambda rows: pl.BlockSpec((None, rows, tm), lambda i: (i // n_s, 0, i % n_s))
    sds = jax.ShapeDtypeStruct
    return pl.pallas_call(
        _inproj_prompt_kernel,
        grid=(t // tm,),
        in_specs=[row(d), _resident((1, d)), _resident(wt.shape), _resident(wk.shape), _resident(wf.shape),
                  _resident(bf.shape), _resident(wret.shape), tab, tab],
        out_specs=[fmajor(w), fmajor(w), fmajor(w), row(w),
                   pl.BlockSpec((None, tm // tk, w, tk), lambda i: (i // n_s, i % n_s, 0, 0)),
                   row(LANES), fmajor(N_HEADS), row(w), row(w), row(w), row(w)],
        out_shape=[sds((batch, w, s), BF16), sds((batch, w, s), F32), sds((batch, w, s), F32), sds((t, w), BF16),
                   sds((batch, s // tk, w, tk), BF16), sds((t, LANES), F32), sds((batch, N_HEADS, s), F32),
                   sds((t, w), BF16), sds((t, w), BF16), sds((t, w), BF16), sds((t, w), F32)],
        compiler_params=_params("parallel"),
        name="inproj_prompt",
    )(x, g, wt, wk, wf, bf, wret, cos, sin)


BIAS_TERMS = 3


def _cumsum_kernel(lf_ref, lft_ref, fb_ref, frow_ref, carry_col, carry_row, *, tc):
    @pl.when(pl.program_id(1) == 0)
    def _():
        carry_col[...] = jnp.zeros_like(carry_col)
        carry_row[...] = jnp.zeros_like(carry_row)

    r = lax.broadcasted_iota(jnp.int32, (tc, tc), 0)
    c = lax.broadcasted_iota(jnp.int32, (tc, tc), 1)
    lower = jnp.where(c <= r, 1.0, 0.0).astype(BF16)
    fcol = _dot_exact01(lf_ref[...], lower, w_left=True) + carry_col[...]
    carry_col[...] = fcol[tc - 1:tc, :]
    rr = lax.broadcasted_iota(jnp.int32, (LANES, LANES), 0)
    cc = lax.broadcasted_iota(jnp.int32, (LANES, LANES), 1)
    fb = jnp.zeros((tc, LANES), F32)
    for j, term in enumerate(_split3(-fcol)):
        place = jnp.where((cc == BIAS_TERMS * rr + j) & (rr < N_HEADS), 1.0, 0.0).astype(BF16)
        fb = fb + _dot(term, place)
    fb_ref[...] = fb.astype(BF16)
    upper = jnp.where(r <= c, 1.0, 0.0).astype(BF16)
    frow = _dot_exact01(lft_ref[...], upper) + carry_row[...]
    frow_ref[...] = frow
    carry_row[...] = frow[:, tc - 1:tc]


def _cumsum(lf_pad, lf_t, *, tc):
    b, s, _ = lf_pad.shape
    col = pl.BlockSpec((None, tc, LANES), lambda i, j: (i, j, 0))
    rowspec = pl.BlockSpec((None, N_HEADS, tc), lambda i, j: (i, 0, j))
    return pl.pallas_call(
        functools.partial(_cumsum_kernel, tc=tc),
        grid=(b, s // tc),
        in_specs=[col, rowspec],
        out_specs=[col, rowspec],
        out_shape=[jax.ShapeDtypeStruct((b, s, LANES), BF16), jax.ShapeDtypeStruct((b, N_HEADS, s), F32)],
        scratch_shapes=[pltpu.VMEM((1, LANES), F32), pltpu.VMEM((N_HEADS, 1), F32)],
        compiler_params=_params("parallel", "arbitrary"),
        name="logf_cumsum",
    )(lf_pad, lf_t)


def _fox_prompt_kernel(qt_ref, k_ref, fb_ref, vt_ref, frow_ref, o_ref, *, tq):
    pg = pl.program_id(1)
    qi = pl.program_id(2)
    pairs = qt_ref.shape[0] // LANES
    heads = 2 * pairs
    row = lax.broadcasted_iota(jnp.int32, (LANES, tq), 0)
    qw, fq = [], []
    for p in range(pairs):
        qt = qt_ref[p * LANES:(p + 1) * LANES, :].astype(F32)
        both = []
        for h in (0, 1):
            head = 2 * (pg * pairs + p) + h
            own = (row < HEAD_DIM) if h == 0 else (row >= HEAD_DIM)
            first = BIAS_TERMS * head
            ones = jnp.where((row >= first) & (row < first + BIAS_TERMS), 1.0, 0.0)
            both.append(jnp.concatenate([jnp.where(own, qt, 0.0), ones], axis=0).astype(BF16))
            fq.append(frow_ref[pl.ds(head, 1), :])
        qw.append(jnp.concatenate(both, axis=1))
    fq = jnp.concatenate(fq, axis=1)

    def scores(kj):
        start = pl.multiple_of(kj * tq, tq)
        bias = fb_ref[pl.ds(start, tq), :]
        return jnp.concatenate(
            [_dot(jnp.concatenate([k_ref[pl.ds(start, tq), p * LANES:(p + 1) * LANES], bias], axis=1), qw[p])
             for p in range(pairs)], axis=1)

    def consume(kj, t, stats, masked):
        m_old, l_old, acc = stats
        if masked:
            key = lax.broadcasted_iota(jnp.int32, t.shape, 0)
            qry = lax.broadcasted_iota(jnp.int32, t.shape, 1) & (tq - 1)
            t = jnp.where(key <= qry, t, -jnp.inf)
        m_new = jnp.maximum(m_old, jnp.max(t, axis=0, keepdims=True) + fq)
        p = jnp.exp(t + (fq - m_new))
        alpha = jnp.exp(m_old - m_new)
        pb = p.astype(BF16)
        pv = jnp.concatenate([_dot(vt_ref[kj, h * HEAD_DIM:(h + 1) * HEAD_DIM, :], pb[:, h * tq:(h + 1) * tq])
                              for h in range(heads)], axis=1)
        return m_new, alpha * l_old + jnp.sum(p, axis=0, keepdims=True), alpha * acc + pv

    def body(kj, carry):
        t_cur, stats = carry
        t_next = scores(kj + 1)
        return t_next, consume(kj, t_cur, stats, False)

    init = (jnp.full((1, heads * tq), -1e30, F32), jnp.zeros((1, heads * tq), F32),
            jnp.zeros((HEAD_DIM, heads * tq), F32))
    t_last, stats = lax.fori_loop(0, qi, body, (scores(0), init))
    _, l_fin, acc = consume(qi, t_last, stats, True)
    out = acc / l_fin
    o_ref[...] = jnp.concatenate([out[:, h * tq:(h + 1) * tq] for h in range(heads)], axis=0).T.astype(o_ref.dtype)


def _fox_prompt(qt, k, fb, vtb, frow, *, tq, pairs):
    b, w, s = qt.shape
    wb = pairs * LANES
    nk = s // tq
    assert tq & (tq - 1) == 0
    return pl.pallas_call(
        functools.partial(_fox_prompt_kernel, tq=tq),
        grid=(b, w // wb, nk),
        in_specs=[pl.BlockSpec((None, wb, tq), lambda i, p, j: (i, p, j)),
                  pl.BlockSpec((None, s, wb), lambda i, p, j: (i, 0, p)),
                  pl.BlockSpec((None, s, LANES), lambda i, p, j: (i, 0, 0)),
                  pl.BlockSpec((None, nk, wb, tq), lambda i, p, j: (i, 0, p, 0)),
                  pl.BlockSpec((None, N_HEADS, tq), lambda i, p, j: (i, 0, j))],
        out_specs=pl.BlockSpec((None, tq, wb), lambda i, p, j: (i, j, p)),
        out_shape=jax.ShapeDtypeStruct((b, s, w), BF16),
        compiler_params=_params("parallel", "parallel", "arbitrary"),
        name="fox_prompt",
    )(qt, k, fb, vtb, frow)


def _ret_kernel(q_ref, k_ref, v_ref, gate_ref, gret_ref, st0_ref, dec_ref, cross_ref, kdec_ref, gam_ref,
                o_ref, st_ref, state, *, n_chunks):
    c = RET_CHUNK
    j = pl.program_id(1)

    @pl.when(j == 0)
    def _():
        state[...] = st0_ref[...]

    lane = lax.broadcasted_iota(jnp.int32, (c, LANES), 1)
    lo = lane < HEAD_DIM
    r = lax.broadcasted_iota(jnp.int32, (LANES, LANES), 0)
    cc = lax.broadcasted_iota(jnp.int32, (LANES, LANES), 1)
    same_head = (r < HEAD_DIM) == (cc < HEAD_DIM)
    mean_w = jnp.where(same_head, 1.0 / HEAD_DIM, 0.0).astype(BF16)
    groups = range(GROUP_WIDTH // LANES)
    lanes = [slice(g * LANES, (g + 1) * LANES) for g in groups]
    for ci in range(n_chunks):
        rows = slice(ci * c, (ci + 1) * c)
        qg = [q_ref[rows, ls] for ls in lanes]
        kg = [k_ref[rows, ls] for ls in lanes]
        vg = [v_ref[rows, ls] for ls in lanes]
        st = [state[g] for g in groups]
        zero = jnp.zeros_like(qg[0])
        s0 = [_dot_nt(jnp.where(lo, qg[g], zero), kg[g]) * dec_ref[2 * g] for g in groups]
        s1 = [_dot_nt(jnp.where(lo, zero, qg[g]), kg[g]) * dec_ref[2 * g + 1] for g in groups]
        o_cross = [_dot(qg[g], st[g].astype(BF16)) * cross_ref[:, lanes[g]] for g in groups]
        upd = [_dot((kg[g].astype(F32) * kdec_ref[:, lanes[g]]).T.astype(BF16), vg[g]) for g in groups]
        for g in groups:
            state[g] = gam_ref[g] * st[g] + jnp.where(same_head, upd[g], 0.0)
        o = [jnp.where(lo, _dot(s0[g].astype(BF16), vg[g]), _dot(s1[g].astype(BF16), vg[g])) + o_cross[g]
             for g in groups]
        outs = []
        for g in groups:
            o2 = o[g] * o[g]
            hi = o2.astype(BF16)
            ms = _dot(hi, mean_w) + _dot((o2 - hi.astype(F32)).astype(BF16), mean_w)
            of = o[g] * lax.rsqrt(ms + EPS)
            outs.append(((of * gret_ref[:, lanes[g]]) * _silu(gate_ref[rows, lanes[g]])).astype(BF16))
        o_ref[rows, :] = jnp.concatenate(outs, axis=1)

    @pl.when(j == pl.num_programs(1) - 1)
    def _():
        st_ref[...] = state[...]


def _retention(q, k, v, gate, gret, st0, tables, *, n_chunks):
    b, s, w = q.shape
    tr = n_chunks * RET_CHUNK
    groups = w // LANES
    dec, cross, kdec, gam = tables
    tok = pl.BlockSpec((None, tr, w), lambda i, j: (i, j, 0))
    stspec = pl.BlockSpec((None, groups, LANES, LANES), lambda i, j: (i, 0, 0, 0))
    return pl.pallas_call(
        functools.partial(_ret_kernel, n_chunks=n_chunks),
        grid=(b, s // tr),
        in_specs=[tok, tok, tok, tok, _resident((1, w)), stspec, _resident(dec.shape), _resident(cross.shape),
                  _resident(kdec.shape), _resident(gam.shape)],
        out_specs=[tok, stspec],
        out_shape=[jax.ShapeDtypeStruct((b, s, w), BF16), jax.ShapeDtypeStruct((b, groups, LANES, LANES), F32)],
        scratch_shapes=[pltpu.VMEM((groups, LANES, LANES), F32)],
        compiler_params=_params("parallel", "arbitrary"),
        name="retention",
    )(q, k, v, gate, gret, st0, dec, cross, kdec, gam)


def _ret_tables(chunk_len):
    c = RET_CHUNK
    log_gamma = jnp.log(1.0 - 2.0 ** (-5.0 - jnp.arange(N_HEADS, dtype=F32)))
    idx = jnp.arange(c, dtype=F32)
    diff = idx[:, None] - idx[None, :]
    dec = jnp.where(diff >= 0, jnp.exp(jnp.maximum(diff, 0.0)[None] * log_gamma[:, None, None]), 0.0)
    cross = jnp.exp((idx + 1.0)[:, None] * log_gamma[None, :])
    kdec = jnp.exp((chunk_len - 1.0 - idx)[:, None] * log_gamma[None, :])
    sdec = jnp.exp(chunk_len * log_gamma)
    head_of_lane = jnp.arange(GROUP_WIDTH) // HEAD_DIM
    row_head = head_of_lane.reshape(GROUP_WIDTH // LANES, LANES)
    same = (jnp.arange(LANES)[:, None] // HEAD_DIM) == (jnp.arange(LANES)[None, :] // HEAD_DIM)
    gam = jnp.where(same[None], sdec[row_head][:, :, None], 0.0)
    return dec, cross[:, head_of_lane], kdec[:, head_of_lane], gam


def _ret_sample_kernel(q_ref, k_ref, v_ref, gate_ref, gret_ref, tab_ref, st_ref, o_ref, sto_ref):
    n_new = q_ref.shape[0]
    const = lambda i: tab_ref[i:i + 1, :]
    q = [q_ref[l] for l in range(n_new)]
    k = [k_ref[l] for l in range(n_new)]
    v = [v_ref[l] for l in range(n_new)]
    inner = []
    for l in range(n_new):
        o = jnp.zeros_like(v[0])
        for m in range(l + 1):
            o = o + (jnp.sum(q[l] * k[m], axis=0, keepdims=True) * const(l - m)) * v[m]
        inner.append(o)
    state_decay = const(3 * n_new)

    def per_key_dim(d, cross):
        s_d = st_ref[d]
        new = state_decay * s_d
        out = []
        for l in range(n_new):
            out.append(cross[l] + q_ref[l, pl.ds(d, 1), :] * s_d)
            new = new + (k_ref[l, pl.ds(d, 1), :] * const(2 * n_new + l)) * v[l]
        sto_ref[d] = new
        return tuple(out)

    cross = lax.fori_loop(0, st_ref.shape[0], per_key_dim, tuple(jnp.zeros_like(v[0]) for _ in range(n_new)))
    for l in range(n_new):
        o = inner[l] + cross[l] * const(n_new + l)
        of = o * lax.rsqrt(jnp.mean(o * o, axis=0, keepdims=True) + EPS)
        o_ref[l] = ((of * gret_ref[...]) * _silu(gate_ref[l])).astype(o_ref.dtype)


def _ret_sample(q, k, v, gate, gret, tab, st):
    n_new, nh, hd, nb = q.shape
    tok = pl.BlockSpec((n_new, None, hd, nb), lambda h: (0, h, 0, 0))
    stspec = pl.BlockSpec((None, hd, hd, nb), lambda h: (h, 0, 0, 0))
    return pl.pallas_call(
        _ret_sample_kernel,
        grid=(nh,),
        in_specs=[tok, tok, tok, tok, pl.BlockSpec((None, hd, 1), lambda h: (h, 0, 0)),
                  pl.BlockSpec((None,) + tab.shape[1:], lambda h: (h, 0, 0)), stspec],
        out_specs=[tok, stspec],
        out_shape=[jax.ShapeDtypeStruct(q.shape, BF16), jax.ShapeDtypeStruct(st.shape, F32)],
        compiler_params=_params("parallel"),
        name="retention_sample",
    )(q, k, v, gate, gret, tab, st)


def _ret_sample_table(n_new, lanes):
    log_gamma = jnp.log(1.0 - 2.0 ** (-5.0 - jnp.arange(N_HEADS, dtype=F32)))[:, None]
    idx = jnp.arange(n_new, dtype=F32)[None, :]
    cols = jnp.concatenate([jnp.exp(idx * log_gamma), jnp.exp((idx + 1.0) * log_gamma),
                            jnp.exp((n_new - 1.0 - idx) * log_gamma), jnp.exp(n_new * log_gamma)], axis=1)
    rows = -(-cols.shape[1] // 8) * 8
    cols = jnp.pad(cols, ((0, 0), (0, rows - cols.shape[1])))
    return jnp.broadcast_to(cols[:, :, None], (N_HEADS, rows, lanes))


def _head_states(bd):
    b = bd.shape[0]
    a = bd[:, :, :HEAD_DIM, :HEAD_DIM]
    d = bd[:, :, HEAD_DIM:, HEAD_DIM:]
    return jnp.stack([a, d], axis=2).reshape(b, N_HEADS, HEAD_DIM, HEAD_DIM)


def _fox_sample_kernel(pt_ref, q_ref, kn_ref, vn_ref, lfn_ref, *rest, pp):
    k_refs = rest[:pp]
    v_refs = rest[pp:2 * pp]
    lf_refs = rest[2 * pp:3 * pp]
    o_ref = rest[3 * pp]
    qbd_s, m_s, l_s, acc_s, carry_s, fq_s = rest[3 * pp + 1:]
    del pt_ref
    j = pl.program_id(1)
    n_new, w = q_ref.shape
    rows = n_new * N_HEADS
    page = k_refs[0].shape[1]
    sub = lax.broadcasted_iota(jnp.int32, (N_HEADS, w), 0)
    own = (lax.broadcasted_iota(jnp.int32, (N_HEADS, w), 1) // HEAD_DIM) == sub
    tr = lax.broadcasted_iota(jnp.int32, (page, page), 0)
    tc = lax.broadcasted_iota(jnp.int32, (page, page), 1)
    tile_rows = lambda x: jnp.concatenate([x] * n_new, axis=0)

    @pl.when(j == 0)
    def _():
        q = q_ref[...]
        qbd = jnp.concatenate([jnp.where(own, jnp.broadcast_to(q[t:t + 1, :], (N_HEADS, w)), 0.0)
                               for t in range(n_new)], axis=0).astype(BF16)
        qbd_s[...] = qbd
        incl = jnp.where(tr <= tc, 1.0, 0.0).astype(BF16)
        cum = _dot_exact01(lfn_ref[...], incl)
        fq = jnp.concatenate([cum[:, t:t + 1] for t in range(n_new)], axis=0)
        fq_s[...] = fq
        new_rows = 16
        pad = jnp.zeros((new_rows - n_new, w), F32)
        kn = jnp.concatenate([kn_ref[...], pad], axis=0).astype(BF16)
        vn = jnp.concatenate([vn_ref[...], pad], axis=0).astype(BF16)
        s = _dot_nt(qbd, kn)
        t_of_row = lax.broadcasted_iota(jnp.int32, (rows, new_rows), 0) // N_HEADS
        m_idx = lax.broadcasted_iota(jnp.int32, (rows, new_rows), 1)
        logits = jnp.where(m_idx <= t_of_row, s + fq - tile_rows(cum[:, :new_rows]), -jnp.inf)
        m0 = jnp.max(logits, axis=1, keepdims=True)
        p = jnp.exp(logits - m0)
        m_s[...] = m0
        l_s[...] = jnp.sum(p, axis=1, keepdims=True)
        acc_s[...] = _dot(p.astype(BF16), vn)
        carry_s[...] = jnp.zeros_like(carry_s)

    lfs = [lf_refs[i][...] for i in range(pp)]
    strict = jnp.where(tr > tc, 1.0, 0.0).astype(BF16)
    suffix = _dot_exact01(jnp.concatenate(lfs, axis=0), strict)
    carry = carry_s[...]
    biases = []
    for i in range(pp):
        d_i = suffix[i * N_HEADS:(i + 1) * N_HEADS]
        biases.append(tile_rows(d_i + carry))
        carry = carry + d_i[:, 0:1] + lfs[i][:, 0:1]
    carry_s[...] = carry
    kcat = jnp.concatenate([k_refs[i][...].astype(BF16) for i in range(pp)], axis=1)
    t = _dot(qbd_s[...], kcat) + jnp.concatenate(biases, axis=1)
    fq = fq_s[...]
    m_old = m_s[...]
    m_new = jnp.maximum(m_old, jnp.max(t, axis=1, keepdims=True) + fq)
    p = jnp.exp(t + (fq - m_new))
    alpha = jnp.exp(m_old - m_new)
    vcat = jnp.concatenate([v_refs[i][...].astype(BF16) for i in range(pp)], axis=1)
    l_new = alpha * l_s[...] + jnp.sum(p, axis=1, keepdims=True)
    acc = alpha * acc_s[...] + _dot_nt(p.astype(BF16), vcat)
    m_s[...] = m_new
    l_s[...] = l_new
    acc_s[...] = acc

    @pl.when(j == pl.num_programs(1) - 1)
    def _():
        out = acc / l_new
        o_ref[...] = jnp.concatenate(
            [jnp.sum(jnp.where(own, out[t * N_HEADS:(t + 1) * N_HEADS], 0.0), axis=0, keepdims=True)
             for t in range(n_new)], axis=0)


def _fox_sample(page_table, q, k_new, v_new, lf_new_t, cache_k_t, cache_v_t, cache_lf_t, *, pp):
    b, n_new, w = q.shape
    n_pages = page_table.shape[1]
    page = cache_k_t.shape[2]
    rows = n_new * N_HEADS
    per_row = lambda shape: pl.BlockSpec((None,) + shape, lambda i, j, pt: (i, 0, 0))

    def paged(shape, slot):
        return pl.BlockSpec((None,) + shape, lambda i, j, pt: (pt[i, n_pages - 1 - (j * pp + slot)], 0, 0))

    kv = [paged((w, page), s) for s in range(pp)]
    lf = [paged((N_HEADS, page), s) for s in range(pp)]
    grid_spec = pltpu.PrefetchScalarGridSpec(
        num_scalar_prefetch=1,
        grid=(b, n_pages // pp),
        in_specs=[per_row((n_new, w)), per_row((n_new, w)), per_row((n_new, w)), per_row((N_HEADS, page))] + kv + kv + lf,
        out_specs=per_row((n_new, w)),
        scratch_shapes=[pltpu.VMEM((rows, w), BF16), pltpu.VMEM((rows, 1), F32), pltpu.VMEM((rows, 1), F32),
                        pltpu.VMEM((rows, w), F32), pltpu.VMEM((N_HEADS, 1), F32), pltpu.VMEM((rows, 1), F32)],
    )
    return pl.pallas_call(
        functools.partial(_fox_sample_kernel, pp=pp),
        grid_spec=grid_spec,
        out_shape=jax.ShapeDtypeStruct(q.shape, F32),
        compiler_params=_params("parallel", "arbitrary"),
        name="fox_sample_paged",
    )(page_table, q, k_new, v_new, lf_new_t, *([cache_k_t] * pp), *([cache_v_t] * pp), *([cache_lf_t] * pp))


def _outproj_kernel(x_ref, of_ref, or_ref, wo_ref, g_ref, wq_ref, x2_ref, q_ref):
    w = GROUP_WIDTH
    x2 = x_ref[...] + _dot(of_ref[...], wo_ref[0:w, :]) + _dot(or_ref[...], wo_ref[w:2 * w, :])
    x2_ref[...] = x2
    q_ref[...] = _dot(_rms(x2, g_ref[...]).astype(BF16), wq_ref[...]) * Q_SCALE


def _outproj(x, o_fox, o_ret, w_out, g_cross, w_cq, *, tm):
    t, d = x.shape
    w = GROUP_WIDTH
    mw = w_cq.shape[1]
    row = lambda width: pl.BlockSpec((tm, width), lambda i: (i, 0))
    return pl.pallas_call(
        _outproj_kernel,
        grid=(t // tm,),
        in_specs=[row(d), row(w), row(w), _resident(w_out.shape), _resident((1, d)), _resident(w_cq.shape)],
        out_specs=[row(d), row(mw)],
        out_shape=[jax.ShapeDtypeStruct((t, d), F32), jax.ShapeDtypeStruct((t, mw), F32)],
        compiler_params=_params("parallel"),
        name="outproj",
    )(x, o_fox, o_ret, w_out, g_cross, w_cq)


def _xattn_kernel(x_ref, q_ref, mk_ref, mv_ref, wo_ref, o_ref, *, group):
    tt, mw = q_ref.shape
    q = q_ref[...]
    lane = lax.broadcasted_iota(jnp.int32, (tt, mw), 1)
    qs = jnp.concatenate([jnp.where(lane // HEAD_DIM == h, q, 0.0) for h in range(N_MEM_HEADS)], axis=0)
    if len(mk_ref.shape) == 3:
        keys_of = lambda ref: jnp.concatenate([ref[i].astype(BF16) for i in range(ref.shape[0])], axis=1)
        s = _dot(qs.astype(BF16), keys_of(mk_ref))
        values = lambda p: _dot_nt(p, keys_of(mv_ref))
    else:
        s = _dot_nt(qs.astype(BF16), mk_ref[...].astype(BF16))
        values = lambda p: _dot(p, mv_ref[...].astype(BF16))
    if group is not None:
        tok_b = (lax.broadcasted_iota(jnp.int32, s.shape, 0) % tt) // group[0]
        key_b = lax.broadcasted_iota(jnp.int32, s.shape, 1) // group[1]
        s = jnp.where(tok_b == key_b, s, -jnp.inf)
    e = jnp.exp(s - jnp.max(s, axis=1, keepdims=True))
    p = (e / jnp.sum(e, axis=1, keepdims=True)).astype(BF16)
    pv = values(p)
    o = jnp.zeros((tt, mw), F32)
    for h in range(N_MEM_HEADS):
        o = o + jnp.where(lane // HEAD_DIM == h, pv[h * tt:(h + 1) * tt], 0.0)
    o_ref[...] = x_ref[...] + _dot(o.astype(BF16), wo_ref[...])


def _xattn(x, q, mk, mv, w_co, *, tt, keys_per_block, mem_index, group):
    t, d = x.shape
    mw = q.shape[1]
    row = lambda width: pl.BlockSpec((tt, width), lambda i: (i, 0))
    if mk.ndim == 3:
        mem = pl.BlockSpec((keys_per_block,) + mk.shape[1:], lambda i: (mem_index(i), 0, 0))
    else:
        mem = pl.BlockSpec((keys_per_block, mw), lambda i: (mem_index(i), 0))
    return pl.pallas_call(
        functools.partial(_xattn_kernel, group=group),
        grid=(t // tt,),
        in_specs=[row(d), row(mw), mem, mem, _resident(w_co.shape)],
        out_specs=row(d),
        out_shape=jax.ShapeDtypeStruct((t, d), F32),
        compiler_params=_params("parallel"),
        name="cross_attention",
    )(x, q, mk, mv, w_co)


def _memkv_kernel(m_ref, g_ref, wk_ref, wv_ref, k_ref, v_ref):
    h = _rms(m_ref[...], g_ref[...]).astype(BF16)
    k_ref[...] = _dot(h, wk_ref[...])
    v_ref[...] = _dot(h, wv_ref[...])


def _memkv(mem, g, wk, wv, *, tm):
    t, d = mem.shape
    mw = wk.shape[1]
    row = lambda width: pl.BlockSpec((tm, width), lambda i: (i, 0))
    return pl.pallas_call(
        _memkv_kernel,
        grid=(t // tm,),
        in_specs=[row(d), _resident((1, d)), _resident(wk.shape), _resident(wv.shape)],
        out_specs=[row(mw), row(mw)],
        out_shape=[jax.ShapeDtypeStruct((t, mw), F32)] * 2,
        compiler_params=_params("parallel"),
        name="memory_kv",
    )(mem, g, wk, wv)


def _rope_tables(pos):
    half = HEAD_DIM // 2
    inv = ROPE_THETA ** (-jnp.arange(half, dtype=F32) / half)
    ang = pos.astype(F32)[:, None] * inv[None, :]
    cos = jnp.cos(ang)
    sin = jnp.sin(ang)
    return (jnp.tile(jnp.concatenate([cos, cos], axis=-1), (1, N_HEADS)),
            jnp.tile(jnp.concatenate([-sin, sin], axis=-1), (1, N_HEADS)))


def _token_tile(t):
    return 512 if t % 512 == 0 else t


def kernel(x_prompt, x_sample, mem_prompt, cache_fox_k, cache_fox_v, cache_fox_logf, state_ret, cache_mem_k, cache_mem_v, page_table, g_ffn1, w1_gate, w1_up, w1_down, g_mix, w_in, b_f, g_ret, w_out, g_cross, g_mem, w_cq, w_ck, w_cv, w_co, g_ffn2, w2_gate, w2_up, w2_down, g_final):
    depth = w_in.shape[0]
    assert depth == 1, "single-layer trunk"
    b, s, d = x_prompt.shape
    db, ds, _ = x_sample.shape
    n_mem = mem_prompt.shape[1]
    page = cache_fox_k.shape[2]
    past_len = page_table.shape[1] * page
    w = GROUP_WIDTH
    assert s % RET_CHUNK == 0 and ds <= RET_CHUNK

    row = lambda a: a.reshape(1, -1)
    bf = lambda a: a.astype(BF16)
    wi = w_in[0]
    fox_end = 3 * w
    wfox = bf(wi[:, :fox_end])
    wf = bf(jnp.pad(wi[:, fox_end:fox_end + N_HEADS], ((0, 0), (0, LANES - N_HEADS))))
    bfp = jnp.pad(row(b_f[0]), ((0, 0), (0, LANES - N_HEADS)))
    ret0 = fox_end + N_HEADS
    wq_r, wk_r, wv_r, wg_r = (wi[:, ret0 + i * w:ret0 + (i + 1) * w] for i in range(4))
    lane = jnp.arange(w)
    swap = jnp.where(lane % HEAD_DIM < HEAD_DIM // 2, lane + HEAD_DIM // 2, lane - HEAD_DIM // 2)
    wret = bf(jnp.concatenate([wq_r, wq_r[:, swap], wk_r, wk_r[:, swap], wv_r, wg_r], axis=1))
    w1 = (bf(w1_gate[0]), bf(w1_up[0]), bf(w1_down[0]))
    w2 = (bf(w2_gate[0]), bf(w2_up[0]), bf(w2_down[0]))
    wo, wcq, wck, wcv, wco = bf(w_out[0]), bf(w_cq[0]), bf(w_ck[0]), bf(w_cv[0]), bf(w_co[0])
    gf = row(g_final)

    def trunk_in(x, pos):
        tm = _token_tile(x.shape[0])
        x1 = _ffn(x, row(g_ffn1[0]), *w1, gf, final_norm=False, tm=tm)
        cos, sin = _rope_tables(pos)
        return (x1,) + tuple(_inproj(x1, row(g_mix[0]), wfox, wf, bfp, wret, cos, sin, tm=tm))

    def trunk_out(x1, o_fox, o_ret, mk, mv, *, tt, keys_per_block, mem_index, group):
        tm = _token_tile(x1.shape[0])
        x2, qc = _outproj(x1, o_fox, o_ret, wo, row(g_cross[0]), wcq, tm=tm)
        x3 = _xattn(x2, qc, mk, mv, wco, tt=tt, keys_per_block=keys_per_block, mem_index=mem_index, group=group)
        return _ffn(x3, row(g_ffn2[0]), *w2, gf, final_norm=True, tm=tm)

    tq = 256
    tm_p = _token_tile(b * s)
    xp1 = _ffn(x_prompt.reshape(b * s, d), row(g_ffn1[0]), *w1, gf, final_norm=False, tm=tm_p)
    cos_p, sin_p = _rope_tables(jnp.arange(s, dtype=jnp.int32))
    wi_t = wi.T
    qt, kt, vt, kab, vtb, lfp, lft, qr, kr, vr, gr = _inproj_prompt(
        xp1, row(g_mix[0]), bf(wi_t[:fox_end]), bf(wi[:, w:2 * w]), wf, bfp, wret, cos_p, sin_p,
        batch=b, tm=tm_p, tk=tq)
    fb, frow = _cumsum(lfp.reshape(b, s, LANES), lft, tc=512)
    o_fox = _fox_prompt(qt, kab.reshape(b, s, w), fb, vtb, frow, tq=tq, pairs=2)
    st_zero = jnp.zeros((b, w // LANES, LANES, LANES), F32)
    o_ret, st_p = _retention(qr.reshape(b, s, w), kr.reshape(b, s, w), vr.reshape(b, s, w), gr.reshape(b, s, w),
                             row(g_ret[0]), st_zero, _ret_tables(RET_CHUNK), n_chunks=4)
    mk, mv = _memkv(mem_prompt.reshape(b * n_mem, d), row(g_mem[0]), wck, wcv, tm=_token_tile(b * n_mem))
    tt = 512
    y_prompt = trunk_out(xp1, o_fox.reshape(b * s, w), o_ret.reshape(b * s, w), mk, mv, tt=tt,
                         keys_per_block=n_mem, mem_index=lambda i: i // (s // tt), group=None)

    pos_s = past_len + jnp.arange(ds, dtype=jnp.int32)
    xs1, qa_s, ka_s, va_s, _, _, lf_s, qr_s, kr_s, vr_s, gr_s = trunk_in(x_sample.reshape(db * ds, d), jnp.tile(pos_s, db))
    feature_major = lambda c: jnp.transpose(c, (0, 2, 3, 1)).reshape(c.shape[0], w, page)
    new_page = lambda a, width: jnp.pad(jnp.transpose(a.reshape(db, ds, width), (0, 2, 1)),
                                        ((0, 0), (0, 0), (0, page - ds)))
    o_fox_s = _fox_sample(page_table, qa_s.astype(F32).reshape(db, ds, w), ka_s.reshape(db, ds, w),
                          va_s.reshape(db, ds, w), new_page(lf_s, N_HEADS), feature_major(cache_fox_k[0]), feature_major(cache_fox_v[0]),
                          jnp.transpose(cache_fox_logf[0], (0, 2, 1)), pp=min(32, page_table.shape[1]))
    batch_minor = lambda a: jnp.transpose(a.astype(F32).reshape(db, ds, N_HEADS, HEAD_DIM), (1, 2, 3, 0))
    o_ret_t, st_s_t = _ret_sample(batch_minor(qr_s), batch_minor(kr_s), batch_minor(vr_s), batch_minor(gr_s),
                                  g_ret[0].reshape(N_HEADS, HEAD_DIM, 1), _ret_sample_table(ds, db),
                                  jnp.transpose(state_ret[0], (1, 2, 3, 0)))
    o_ret_s = jnp.transpose(o_ret_t, (3, 0, 1, 2)).reshape(db * ds, w)
    bb = 8
    mem_feature_major = lambda c: jnp.transpose(c, (0, 2, 3, 1)).reshape(db, -1, n_mem)
    y_sample = trunk_out(xs1, bf(o_fox_s.reshape(db * ds, w)), o_ret_s,
                         mem_feature_major(cache_mem_k[0]), mem_feature_major(cache_mem_v[0]),
                         tt=bb * ds, keys_per_block=bb, mem_index=lambda i: i, group=(ds, n_mem))

    mem_heads = lambda a: a.reshape(1, b, n_mem, N_MEM_HEADS, -1)
    token_major = lambda a: jnp.transpose(a.reshape(1, b, N_HEADS, HEAD_DIM, s), (0, 1, 4, 2, 3))
    return (y_prompt.reshape(b, s, d), y_sample.reshape(db, ds, d),
            token_major(kt), token_major(vt),
            jnp.transpose(lft, (0, 2, 1))[None], _head_states(st_p)[None], mem_heads(mk), mem_heads(mv),
            ka_s.reshape(1, db, ds, N_HEADS, HEAD_DIM), va_s.reshape(1, db, ds, N_HEADS, HEAD_DIM),
            lf_s.reshape(1, db, ds, N_HEADS), jnp.transpose(st_s_t, (3, 0, 1, 2))[None])
```

```python
import functools

import jax
import jax.numpy as jnp
from jax import lax
from jax.experimental import pallas as pl
from jax.experimental.pallas import tpu as pltpu

F32 = jnp.float32
BF16 = jnp.bfloat16

HEAD_DIM = 64
N_HEADS = 8
GROUP_WIDTH = N_HEADS * HEAD_DIM
N_MEM_HEADS = 4
ROPE_THETA = 10000.0
EPS = 1e-6
FFN_RES = 0.5
Q_SCALE = HEAD_DIM ** -0.5
RET_CHUNK = 128
LANES = 128
VMEM_LIMIT = 56 * 1024 * 1024


def _params(*sem):
    return pltpu.CompilerParams(dimension_semantics=sem, vmem_limit_bytes=VMEM_LIMIT)


def _resident(shape):
    nd = len(shape)
    return pl.BlockSpec(shape, lambda *_: (0,) * nd, pipeline_mode=pl.Buffered(1))


def _dot(a, b):
    return jnp.dot(a, b, preferred_element_type=F32)


def _dot_nt(a, b):
    return lax.dot_general(a, b, (((1,), (1,)), ((), ())), preferred_element_type=F32)


def _rms(x, g):
    return x * lax.rsqrt(jnp.mean(x * x, axis=-1, keepdims=True) + EPS) * g


def _silu(x):
    return x * jax.nn.sigmoid(x)


def _split3(x):
    hi = x.astype(BF16)
    r1 = x - hi.astype(F32)
    mid = r1.astype(BF16)
    lo = (r1 - mid.astype(F32)).astype(BF16)
    return hi, mid, lo


def _dot_exact01(x, w01, w_left=False):
    mm = (lambda part: _dot(w01, part)) if w_left else (lambda part: _dot(part, w01))
    hi, mid, lo = _split3(x)
    return mm(hi) + mm(mid) + mm(lo)


def _ffn_kernel(x_ref, g_ref, wg_ref, wu_ref, wd_ref, gf_ref, o_ref, *, chunks, final_norm):
    x = x_ref[...]
    h = _rms(x, g_ref[...]).astype(BF16)
    acc = jnp.zeros_like(x)
    start = 0
    for width in chunks:
        sl = slice(start, start + width)
        a = _silu(_dot(h, wg_ref[:, sl])) * _dot(h, wu_ref[:, sl])
        acc = acc + _dot(a.astype(BF16), wd_ref[sl, :])
        start += width
    y = x + FFN_RES * acc
    if final_norm:
        y = _rms(y, gf_ref[...])
    o_ref[...] = y


def _ffn(x, g, wg, wu, wd, gf, *, final_norm, tm):
    t, d = x.shape
    d_ff = wg.shape[1]
    chunks = [512] * (d_ff // 512)
    if d_ff % 512:
        chunks.append(d_ff % 512)
    row = pl.BlockSpec((tm, d), lambda i: (i, 0))
    return pl.pallas_call(
        functools.partial(_ffn_kernel, chunks=tuple(chunks), final_norm=final_norm),
        grid=(t // tm,),
        in_specs=[row, _resident((1, d)), _resident(wg.shape), _resident(wu.shape),
                  _resident(wd.shape), _resident((1, d))],
        out_specs=row,
        out_shape=jax.ShapeDtypeStruct((t, d), F32),
        compiler_params=_params("parallel"),
        name="ffn",
    )(x, g, wg, wu, wd, gf)


def _inproj_kernel(x_ref, g_ref, wfox_ref, wf_ref, bf_ref, wret_ref, cos_ref, sin_ref,
                   qa_ref, ka_ref, va_ref, kab_ref, vab_ref, lf_ref, qr_ref, kr_ref, vr_ref, gr_ref):
    h = _rms(x_ref[...], g_ref[...]).astype(BF16)
    w = GROUP_WIDTH
    qa_ref[...] = (_dot(h, wfox_ref[:, 0:w]) * Q_SCALE).astype(BF16)
    ka = _dot(h, wfox_ref[:, w:2 * w])
    ka_ref[...] = ka
    kab_ref[...] = ka.astype(BF16)
    va = _dot(h, wfox_ref[:, 2 * w:3 * w])
    va_ref[...] = va
    vab_ref[...] = va.astype(BF16)
    z = _dot(h, wf_ref[...]) + bf_ref[...]
    lf_ref[...] = (jnp.minimum(z, 0.0) - jnp.log1p(jnp.exp(-jnp.abs(z))))[:, :N_HEADS]
    cos = cos_ref[...]
    sin = sin_ref[...]
    qr_ref[...] = (_dot(h, wret_ref[:, 0:w]) * cos + _dot(h, wret_ref[:, w:2 * w]) * sin).astype(BF16)
    kr = _dot(h, wret_ref[:, 2 * w:3 * w]) * cos + _dot(h, wret_ref[:, 3 * w:4 * w]) * sin
    kr_ref[...] = (kr * Q_SCALE).astype(BF16)
    vr_ref[...] = _dot(h, wret_ref[:, 4 * w:5 * w]).astype(BF16)
    gr_ref[...] = _dot(h, wret_ref[:, 5 * w:6 * w])


def _inproj(x, g, wfox, wf, bf, wret, cos, sin, *, tm):
    t, d = x.shape
    w = GROUP_WIDTH
    n_pos = cos.shape[0] // tm
    row = lambda width: pl.BlockSpec((tm, width), lambda i: (i, 0))
    tab = pl.BlockSpec((tm, w), lambda i: (i % n_pos, 0))
    sds = lambda width, dt: jax.ShapeDtypeStruct((t, width), dt)
    return pl.pallas_call(
        _inproj_kernel,
        grid=(t // tm,),
        in_specs=[row(d), _resident((1, d)), _resident(wfox.shape), _resident(wf.shape),
                  _resident(bf.shape), _resident(wret.shape), tab, tab],
        out_specs=[row(w), row(w), row(w), row(w), row(w), row(N_HEADS), row(w), row(w), row(w), row(w)],
        out_shape=[sds(w, BF16), sds(w, F32), sds(w, F32), sds(w, BF16), sds(w, BF16), sds(N_HEADS, F32),
                   sds(w, BF16), sds(w, BF16), sds(w, BF16), sds(w, F32)],
        compiler_params=_params("parallel"),
        name="inproj",
    )(x, g, wfox, wf, bf, wret, cos, sin)


def _inproj_prompt_kernel(x_ref, g_ref, wt_ref, wk_ref, wf_ref, bf_ref, wret_ref, cos_ref, sin_ref,
                          qt_ref, kt_ref, vt_ref, kb_ref, vtb_ref, lf_ref, lft_ref, qr_ref, kr_ref, vr_ref, gr_ref):
    h = _rms(x_ref[...], g_ref[...]).astype(BF16)
    w = GROUP_WIDTH
    qt_ref[...] = (_dot_nt(wt_ref[0:w, :], h) * Q_SCALE).astype(BF16)
    kt_ref[...] = _dot_nt(wt_ref[w:2 * w, :], h)
    vt = _dot_nt(wt_ref[2 * w:3 * w, :], h)
    vt_ref[...] = vt
    tk = vtb_ref.shape[-1]
    for c in range(vtb_ref.shape[0]):
        vtb_ref[c] = vt[:, c * tk:(c + 1) * tk].astype(BF16)
    kb_ref[...] = _dot(h, wk_ref[...]).astype(BF16)
    z = _dot(h, wf_ref[...]) + bf_ref[...]
    lane = lax.broadcasted_iota(jnp.int32, z.shape, 1)
    lf = jnp.where(lane < N_HEADS, jnp.minimum(z, 0.0) - jnp.log1p(jnp.exp(-jnp.abs(z))), 0.0)
    lf_ref[...] = lf
    lft_ref[...] = lf.T[:N_HEADS, :]
    cos = cos_ref[...]
    sin = sin_ref[...]
    qr_ref[...] = (_dot(h, wret_ref[:, 0:w]) * cos + _dot(h, wret_ref[:, w:2 * w]) * sin).astype(BF16)
    kr = _dot(h, wret_ref[:, 2 * w:3 * w]) * cos + _dot(h, wret_ref[:, 3 * w:4 * w]) * sin
    kr_ref[...] = (kr * Q_SCALE).astype(BF16)
    vr_ref[...] = _dot(h, wret_ref[:, 4 * w:5 * w]).astype(BF16)
    gr_ref[...] = _dot(h, wret_ref[:, 5 * w:6 * w])


def _inproj_prompt(x, g, wt, wk, wf, bf, wret, cos, sin, *, batch, tm, tk):
    t, d = x.shape
    w = GROUP_WIDTH
    s = t // batch
    n_s = s // tm
    row = lambda width: pl.BlockSpec((tm, width), lambda i: (i, 0))
    tab = pl.BlockSpec((tm, w), lambda i: (i % n_s, 0))
    fmajor = lambda rows: pl.BlockSpec((None, rows, tm), lambda i: (i // n_s, 0, i % n_s))
    sds = jax.ShapeDtypeStruct
    return pl.pallas_call(
        _inproj_prompt_kernel,
        grid=(t // tm,),
        in_specs=[row(d), _resident((1, d)), _resident(wt.shape), _resident(wk.shape), _resident(wf.shape),
                  _resident(bf.shape), _resident(wret.shape), tab, tab],
        out_specs=[fmajor(w), fmajor(w), fmajor(w), row(w),
                   pl.BlockSpec((None, tm // tk, w, tk), lambda i: (i // n_s, i % n_s, 0, 0)),
                   row(LANES), fmajor(N_HEADS), row(w), row(w), row(w), row(w)],
        out_shape=[sds((batch, w, s), BF16), sds((batch, w, s), F32), sds((batch, w, s), F32), sds((t, w), BF16),
                   sds((batch, s // tk, w, tk), BF16), sds((t, LANES), F32), sds((batch, N_HEADS, s), F32),
                   sds((t, w), BF16), sds((t, w), BF16), sds((t, w), BF16), sds((t, w), F32)],
        compiler_params=_params("parallel"),
        name="inproj_prompt",
    )(x, g, wt, wk, wf, bf, wret, cos, sin)


BIAS_TERMS = 3


def _cumsum_kernel(lf_ref, lft_ref, fb_ref, frow_ref, carry_col, carry_row, *, tc):
    @pl.when(pl.program_id(1) == 0)
    def _():
        carry_col[...] = jnp.zeros_like(carry_col)
        carry_row[...] = jnp.zeros_like(carry_row)

    r = lax.broadcasted_iota(jnp.int32, (tc, tc), 0)
    c = lax.broadcasted_iota(jnp.int32, (tc, tc), 1)
    lower = jnp.where(c <= r, 1.0, 0.0).astype(BF16)
    fcol = _dot_exact01(lf_ref[...], lower, w_left=True) + carry_col[...]
    carry_col[...] = fcol[tc - 1:tc, :]
    rr = lax.broadcasted_iota(jnp.int32, (LANES, LANES), 0)
    cc = lax.broadcasted_iota(jnp.int32, (LANES, LANES), 1)
    fb = jnp.zeros((tc, LANES), F32)
    for j, term in enumerate(_split3(-fcol)):
        place = jnp.where((cc == BIAS_TERMS * rr + j) & (rr < N_HEADS), 1.0, 0.0).astype(BF16)
        fb = fb + _dot(term, place)
    fb_ref[...] = fb.astype(BF16)
    upper = jnp.where(r <= c, 1.0, 0.0).astype(BF16)
    frow = _dot_exact01(lft_ref[...], upper) + carry_row[...]
    frow_ref[...] = frow
    carry_row[...] = frow[:, tc - 1:tc]


def _cumsum(lf_pad, lf_t, *, tc):
    b, s, _ = lf_pad.shape
    col = pl.BlockSpec((None, tc, LANES), lambda i, j: (i, j, 0))
    rowspec = pl.BlockSpec((None, N_HEADS, tc), lambda i, j: (i, 0, j))
    return pl.pallas_call(
        functools.partial(_cumsum_kernel, tc=tc),
        grid=(b, s // tc),
        in_specs=[col, rowspec],
        out_specs=[col, rowspec],
        out_shape=[jax.ShapeDtypeStruct((b, s, LANES), BF16), jax.ShapeDtypeStruct((b, N_HEADS, s), F32)],
        scratch_shapes=[pltpu.VMEM((1, LANES), F32), pltpu.VMEM((N_HEADS, 1), F32)],
        compiler_params=_params("parallel", "arbitrary"),
        name="logf_cumsum",
    )(lf_pad, lf_t)


def _fox_prompt_kernel(qt_ref, k_ref, fb_ref, vt_ref, frow_ref, o_ref, *, tq):
    pg = pl.program_id(1)
    qi = pl.program_id(2)
    pairs = qt_ref.shape[0] // LANES
    heads = 2 * pairs
    row = lax.broadcasted_iota(jnp.int32, (LANES, tq), 0)
    qw, fq = [], []
    for p in range(pairs):
        qt = qt_ref[p * LANES:(p + 1) * LANES, :].astype(F32)
        both = []
        for h in (0, 1):
            head = 2 * (pg * pairs + p) + h
            own = (row < HEAD_DIM) if h == 0 else (row >= HEAD_DIM)
            first = BIAS_TERMS * head
            ones = jnp.where((row >= first) & (row < first + BIAS_TERMS), 1.0, 0.0)
            both.append(jnp.concatenate([jnp.where(own, qt, 0.0), ones], axis=0).astype(BF16))
            fq.append(frow_ref[pl.ds(head, 1), :])
        qw.append(jnp.concatenate(both, axis=1))
    fq = jnp.concatenate(fq, axis=1)

    def scores(kj):
        start = pl.multiple_of(kj * tq, tq)
        bias = fb_ref[pl.ds(start, tq), :]
        return jnp.concatenate(
            [_dot(jnp.concatenate([k_ref[pl.ds(start, tq), p * LANES:(p + 1) * LANES], bias], axis=1), qw[p])
             for p in range(pairs)], axis=1)

    def consume(kj, t, stats, masked):
        m_old, l_old, acc = stats
        if masked:
            key = lax.broadcasted_iota(jnp.int32, t.shape, 0)
            qry = lax.broadcasted_iota(jnp.int32, t.shape, 1) & (tq - 1)
            t = jnp.where(key <= qry, t, -jnp.inf)
        m_new = jnp.maximum(m_old, jnp.max(t, axis=0, keepdims=True) + fq)
        p = jnp.exp(t + (fq - m_new))
        alpha = jnp.exp(m_old - m_new)
        pb = p.astype(BF16)
        pv = jnp.concatenate([_dot(vt_ref[kj, h * HEAD_DIM:(h + 1) * HEAD_DIM, :], pb[:, h * tq:(h + 1) * tq])
                              for h in range(heads)], axis=1)
        return m_new, alpha * l_old + jnp.sum(p, axis=0, keepdims=True), alpha * acc + pv

    def body(kj, carry):
        t_cur, stats = carry
        t_next = scores(kj + 1)
        return t_next, consume(kj, t_cur, stats, False)

    init = (jnp.full((1, heads * tq), -1e30, F32), jnp.zeros((1, heads * tq), F32),
            jnp.zeros((HEAD_DIM, heads * tq), F32))
    t_last, stats = lax.fori_loop(0, qi, body, (scores(0), init))
    _, l_fin, acc = consume(qi, t_last, stats, True)
    out = acc / l_fin
    o_ref[...] = jnp.concatenate([out[:, h * tq:(h + 1) * tq] for h in range(heads)], axis=0).T.astype(o_ref.dtype)


def _fox_prompt(qt, k, fb, vtb, frow, *, tq, pairs):
    b, w, s = qt.shape
    wb = pairs * LANES
    nk = s // tq
    assert tq & (tq - 1) == 0
    return pl.pallas_call(
        functools.partial(_fox_prompt_kernel, tq=tq),
        grid=(b, w // wb, nk),
        in_specs=[pl.BlockSpec((None, wb, tq), lambda i, p, j: (i, p, j)),
                  pl.BlockSpec((None, s, wb), lambda i, p, j: (i, 0, p)),
                  pl.BlockSpec((None, s, LANES), lambda i, p, j: (i, 0, 0)),
                  pl.BlockSpec((None, nk, wb, tq), lambda i, p, j: (i, 0, p, 0)),
                  pl.BlockSpec((None, N_HEADS, tq), lambda i, p, j: (i, 0, j))],
        out_specs=pl.BlockSpec((None, tq, wb), lambda i, p, j: (i, j, p)),
        out_shape=jax.ShapeDtypeStruct((b, s, w), BF16),
        compiler_params=_params("parallel", "parallel", "arbitrary"),
        name="fox_prompt",
    )(qt, k, fb, vtb, frow)


def _ret_kernel(q_ref, k_ref, v_ref, gate_ref, gret_ref, st0_ref, dec_ref, cross_ref, kdec_ref, gam_ref,
                o_ref, st_ref, state, *, n_chunks):
    c = RET_CHUNK
    j = pl.program_id(1)

    @pl.when(j == 0)
    def _():
        state[...] = st0_ref[...]

    lane = lax.broadcasted_iota(jnp.int32, (c, LANES), 1)
    lo = lane < HEAD_DIM
    r = lax.broadcasted_iota(jnp.int32, (LANES, LANES), 0)
    cc = lax.broadcasted_iota(jnp.int32, (LANES, LANES), 1)
    same_head = (r < HEAD_DIM) == (cc < HEAD_DIM)
    mean_w = jnp.where(same_head, 1.0 / HEAD_DIM, 0.0).astype(BF16)
    groups = range(GROUP_WIDTH // LANES)
    lanes = [slice(g * LANES, (g + 1) * LANES) for g in groups]
    for ci in range(n_chunks):
        rows = slice(ci * c, (ci + 1) * c)
        qg = [q_ref[rows, ls] for ls in lanes]
        kg = [k_ref[rows, ls] for ls in lanes]
        vg = [v_ref[rows, ls] for ls in lanes]
        st = [state[g] for g in groups]
        zero = jnp.zeros_like(qg[0])
        s0 = [_dot_nt(jnp.where(lo, qg[g], zero), kg[g]) * dec_ref[2 * g] for g in groups]
        s1 = [_dot_nt(jnp.where(lo, zero, qg[g]), kg[g]) * dec_ref[2 * g + 1] for g in groups]
        o_cross = [_dot(qg[g], st[g].astype(BF16)) * cross_ref[:, lanes[g]] for g in groups]
        upd = [_dot((kg[g].astype(F32) * kdec_ref[:, lanes[g]]).T.astype(BF16), vg[g]) for g in groups]
        for g in groups:
            state[g] = gam_ref[g] * st[g] + jnp.where(same_head, upd[g], 0.0)
        o = [jnp.where(lo, _dot(s0[g].astype(BF16), vg[g]), _dot(s1[g].astype(BF16), vg[g])) + o_cross[g]
             for g in groups]
        outs = []
        for g in groups:
            o2 = o[g] * o[g]
            hi = o2.astype(BF16)
            ms = _dot(hi, mean_w) + _dot((o2 - hi.astype(F32)).astype(BF16), mean_w)
            of = o[g] * lax.rsqrt(ms + EPS)
            outs.append(((of * gret_ref[:, lanes[g]]) * _silu(gate_ref[rows, lanes[g]])).astype(BF16))
        o_ref[rows, :] = jnp.concatenate(outs, axis=1)

    @pl.when(j == pl.num_programs(1) - 1)
    def _():
        st_ref[...] = state[...]


def _retention(q, k, v, gate, gret, st0, tables, *, n_chunks):
    b, s, w = q.shape
    tr = n_chunks * RET_CHUNK
    groups = w // LANES
    dec, cross, kdec, gam = tables
    tok = pl.BlockSpec((None, tr, w), lambda i, j: (i, j, 0))
    stspec = pl.BlockSpec((None, groups, LANES, LANES), lambda i, j: (i, 0, 0, 0))
    return pl.pallas_call(
        functools.partial(_ret_kernel, n_chunks=n_chunks),
        grid=(b, s // tr),
        in_specs=[tok, tok, tok, tok, _resident((1, w)), stspec, _resident(dec.shape), _resident(cross.shape),
                  _resident(kdec.shape), _resident(gam.shape)],
        out_specs=[tok, stspec],
        out_shape=[jax.ShapeDtypeStruct((b, s, w), BF16), jax.ShapeDtypeStruct((b, groups, LANES, LANES), F32)],
        scratch_shapes=[pltpu.VMEM((groups, LANES, LANES), F32)],
        compiler_params=_params("parallel", "arbitrary"),
        name="retention",
    )(q, k, v, gate, gret, st0, dec, cross, kdec, gam)


def _ret_tables(chunk_len):
    c = RET_CHUNK
    log_gamma = jnp.log(1.0 - 2.0 ** (-5.0 - jnp.arange(N_HEADS, dtype=F32)))
    idx = jnp.arange(c, dtype=F32)
    diff = idx[:, None] - idx[None, :]
    dec = jnp.where(diff >= 0, jnp.exp(jnp.maximum(diff, 0.0)[None] * log_gamma[:, None, None]), 0.0)
    cross = jnp.exp((idx + 1.0)[:, None] * log_gamma[None, :])
    kdec = jnp.exp((chunk_len - 1.0 - idx)[:, None] * log_gamma[None, :])
    sdec = jnp.exp(chunk_len * log_gamma)
    head_of_lane = jnp.arange(GROUP_WIDTH) // HEAD_DIM
    row_head = head_of_lane.reshape(GROUP_WIDTH // LANES, LANES)
    same = (jnp.arange(LANES)[:, None] // HEAD_DIM) == (jnp.arange(LANES)[None, :] // HEAD_DIM)
    gam = jnp.where(same[None], sdec[row_head][:, :, None], 0.0)
    return dec, cross[:, head_of_lane], kdec[:, head_of_lane], gam


def _ret_sample_kernel(q_ref, k_ref, v_ref, gate_ref, gret_ref, tab_ref, st_ref, o_ref, sto_ref):
    n_new = q_ref.shape[0]
    const = lambda i: tab_ref[i:i + 1, :]
    q = [q_ref[l] for l in range(n_new)]
    k = [k_ref[l] for l in range(n_new)]
    v = [v_ref[l] for l in range(n_new)]
    inner = []
    for l in range(n_new):
        o = jnp.zeros_like(v[0])
        for m in range(l + 1):
            o = o + (jnp.sum(q[l] * k[m], axis=0, keepdims=True) * const(l - m)) * v[m]
        inner.append(o)
    state_decay = const(3 * n_new)

    def per_key_dim(d, cross):
        s_d = st_ref[d]
        new = state_decay * s_d
        out = []
        for l in range(n_new):
            out.append(cross[l] + q_ref[l, pl.ds(d, 1), :] * s_d)
            new = new + (k_ref[l, pl.ds(d, 1), :] * const(2 * n_new + l)) * v[l]
        sto_ref[d] = new
        return tuple(out)

    cross = lax.fori_loop(0, st_ref.shape[0], per_key_dim, tuple(jnp.zeros_like(v[0]) for _ in range(n_new)))
    for l in range(n_new):
        o = inner[l] + cross[l] * const(n_new + l)
        of = o * lax.rsqrt(jnp.mean(o * o, axis=0, keepdims=True) + EPS)
        o_ref[l] = ((of * gret_ref[...]) * _silu(gate_ref[l])).astype(o_ref.dtype)


def _ret_sample(q, k, v, gate, gret, tab, st):
    n_new, nh, hd, nb = q.shape
    tok = pl.BlockSpec((n_new, None, hd, nb), lambda h: (0, h, 0, 0))
    stspec = pl.BlockSpec((None, hd, hd, nb), lambda h: (h, 0, 0, 0))
    return pl.pallas_call(
        _ret_sample_kernel,
        grid=(nh,),
        in_specs=[tok, tok, tok, tok, pl.BlockSpec((None, hd, 1), lambda h: (h, 0, 0)),
                  pl.BlockSpec((None,) + tab.shape[1:], lambda h: (h, 0, 0)), stspec],
        out_specs=[tok, stspec],
        out_shape=[jax.ShapeDtypeStruct(q.shape, BF16), jax.ShapeDtypeStruct(st.shape, F32)],
        compiler_params=_params("parallel"),
        name="retention_sample",
    )(q, k, v, gate, gret, tab, st)


def _ret_sample_table(n_new, lanes):
    log_gamma = jnp.log(1.0 - 2.0 ** (-5.0 - jnp.arange(N_HEADS, dtype=F32)))[:, None]
    idx = jnp.arange(n_new, dtype=F32)[None, :]
    cols = jnp.concatenate([jnp.exp(idx * log_gamma), jnp.exp((idx + 1.0) * log_gamma),
                            jnp.exp((n_new - 1.0 - idx) * log_gamma), jnp.exp(n_new * log_gamma)], axis=1)
    rows = -(-cols.shape[1] // 8) * 8
    cols = jnp.pad(cols, ((0, 0), (0, rows - cols.shape[1])))
    return jnp.broadcast_to(cols[:, :, None], (N_HEADS, rows, lanes))


def _head_states(bd):
    b = bd.shape[0]
    a = bd[:, :, :HEAD_DIM, :HEAD_DIM]
    d = bd[:, :, HEAD_DIM:, HEAD_DIM:]
    return jnp.stack([a, d], axis=2).reshape(b, N_HEADS, HEAD_DIM, HEAD_DIM)


def _fox_sample_kernel(pt_ref, q_ref, kn_ref, vn_ref, lfn_ref, *rest, pp):
    k_refs = rest[:pp]
    v_refs = rest[pp:2 * pp]
    lf_refs = rest[2 * pp:3 * pp]
    o_ref = rest[3 * pp]
    qbd_s, m_s, l_s, acc_s, carry_s, fq_s = rest[3 * pp + 1:]
    del pt_ref
    j = pl.program_id(1)
    n_new, w = q_ref.shape
    rows = n_new * N_HEADS
    page = k_refs[0].shape[1]
    sub = lax.broadcasted_iota(jnp.int32, (N_HEADS, w), 0)
    own = (lax.broadcasted_iota(jnp.int32, (N_HEADS, w), 1) // HEAD_DIM) == sub
    tr = lax.broadcasted_iota(jnp.int32, (page, page), 0)
    tc = lax.broadcasted_iota(jnp.int32, (page, page), 1)
    tile_rows = lambda x: jnp.concatenate([x] * n_new, axis=0)

    @pl.when(j == 0)
    def _():
        q = q_ref[...]
        qbd = jnp.concatenate([jnp.where(own, jnp.broadcast_to(q[t:t + 1, :], (N_HEADS, w)), 0.0)
                               for t in range(n_new)], axis=0).astype(BF16)
        qbd_s[...] = qbd
        incl = jnp.where(tr <= tc, 1.0, 0.0).astype(BF16)
        cum = _dot_exact01(lfn_ref[...], incl)
        fq = jnp.concatenate([cum[:, t:t + 1] for t in range(n_new)], axis=0)
        fq_s[...] = fq
        new_rows = 16
        pad = jnp.zeros((new_rows - n_new, w), F32)
        kn = jnp.concatenate([kn_ref[...], pad], axis=0).astype(BF16)
        vn = jnp.concatenate([vn_ref[...], pad], axis=0).astype(BF16)
        s = _dot_nt(qbd, kn)
        t_of_row = lax.broadcasted_iota(jnp.int32, (rows, new_rows), 0) // N_HEADS
        m_idx = lax.broadcasted_iota(jnp.int32, (rows, new_rows), 1)
        logits = jnp.where(m_idx <= t_of_row, s + fq - tile_rows(cum[:, :new_rows]), -jnp.inf)
        m0 = jnp.max(logits, axis=1, keepdims=True)
        p = jnp.exp(logits - m0)
        m_s[...] = m0
        l_s[...] = jnp.sum(p, axis=1, keepdims=True)
        acc_s[...] = _dot(p.astype(BF16), vn)
        carry_s[...] = jnp.zeros_like(carry_s)

    lfs = [lf_refs[i][...] for i in range(pp)]
    strict = jnp.where(tr > tc, 1.0, 0.0).astype(BF16)
    suffix = _dot_exact01(jnp.concatenate(lfs, axis=0), strict)
    carry = carry_s[...]
    biases = []
    for i in range(pp):
        d_i = suffix[i * N_HEADS:(i + 1) * N_HEADS]
        biases.append(tile_rows(d_i + carry))
        carry = carry + d_i[:, 0:1] + lfs[i][:, 0:1]
    carry_s[...] = carry
    kcat = jnp.concatenate([k_refs[i][...].astype(BF16) for i in range(pp)], axis=1)
    t = _dot(qbd_s[...], kcat) + jnp.concatenate(biases, axis=1)
    fq = fq_s[...]
    m_old = m_s[...]
    m_new = jnp.maximum(m_old, jnp.max(t, axis=1, keepdims=True) + fq)
    p = jnp.exp(t + (fq - m_new))
    alpha = jnp.exp(m_old - m_new)
    vcat = jnp.concatenate([v_refs[i][...].astype(BF16) for i in range(pp)], axis=1)
    l_new = alpha * l_s[...] + jnp.sum(p, axis=1, keepdims=True)
    acc = alpha * acc_s[...] + _dot_nt(p.astype(BF16), vcat)
    m_s[...] = m_new
    l_s[...] = l_new
    acc_s[...] = acc

    @pl.when(j == pl.num_programs(1) - 1)
    def _():
        out = acc / l_new
        o_ref[...] = jnp.concatenate(
            [jnp.sum(jnp.where(own, out[t * N_HEADS:(t + 1) * N_HEADS], 0.0), axis=0, keepdims=True)
             for t in range(n_new)], axis=0)


def _fox_sample(page_table, q, k_new, v_new, lf_new_t, cache_k_t, cache_v_t, cache_lf_t, *, pp):
    b, n_new, w = q.shape
    n_pages = page_table.shape[1]
    page = cache_k_t.shape[2]
    rows = n_new * N_HEADS
    per_row = lambda shape: pl.BlockSpec((None,) + shape, lambda i, j, pt: (i, 0, 0))

    def paged(shape, slot):
        return pl.BlockSpec((None,) + shape, lambda i, j, pt: (pt[i, n_pages - 1 - (j * pp + slot)], 0, 0))

    kv = [paged((w, page), s) for s in range(pp)]
    lf = [paged((N_HEADS, page), s) for s in range(pp)]
    grid_spec = pltpu.PrefetchScalarGridSpec(
        num_scalar_prefetch=1,
        grid=(b, n_pages // pp),
        in_specs=[per_row((n_new, w)), per_row((n_new, w)), per_row((n_new, w)), per_row((N_HEADS, page))] + kv + kv + lf,
        out_specs=per_row((n_new, w)),
        scratch_shapes=[pltpu.VMEM((rows, w), BF16), pltpu.VMEM((rows, 1), F32), pltpu.VMEM((rows, 1), F32),
                        pltpu.VMEM((rows, w), F32), pltpu.VMEM((N_HEADS, 1), F32), pltpu.VMEM((rows, 1), F32)],
    )
    return pl.pallas_call(
        functools.partial(_fox_sample_kernel, pp=pp),
        grid_spec=grid_spec,
        out_shape=jax.ShapeDtypeStruct(q.shape, F32),
        compiler_params=_params("parallel", "arbitrary"),
        name="fox_sample_paged",
    )(page_table, q, k_new, v_new, lf_new_t, *([cache_k_t] * pp), *([cache_v_t] * pp), *([cache_lf_t] * pp))


def _mix_xattn_kernel(x_ref, of_ref, or_ref, wout_ref, g_ref, wq_ref, mk_ref, mv_ref, wo_ref, o_ref, *, group):
    w = GROUP_WIDTH
    x2 = x_ref[...] + _dot(of_ref[...], wout_ref[0:w, :]) + _dot(or_ref[...], wout_ref[w:2 * w, :])
    q = _dot(_rms(x2, g_ref[...]).astype(BF16), wq_ref[...]) * Q_SCALE
    tt, mw = q.shape
    lane = lax.broadcasted_iota(jnp.int32, (tt, mw), 1)
    qs = jnp.concatenate([jnp.where(lane // HEAD_DIM == h, q, 0.0) for h in range(N_MEM_HEADS)], axis=0)
    if len(mk_ref.shape) == 3:
        keys_of = lambda ref: jnp.concatenate([ref[i].astype(BF16) for i in range(ref.shape[0])], axis=1)
        s = _dot(qs.astype(BF16), keys_of(mk_ref))
        values = lambda p: _dot_nt(p, keys_of(mv_ref))
    else:
        s = _dot_nt(qs.astype(BF16), mk_ref[...].astype(BF16))
        values = lambda p: _dot(p, mv_ref[...].astype(BF16))
    if group is not None:
        tok_b = (lax.broadcasted_iota(jnp.int32, s.shape, 0) % tt) // group[0]
        key_b = lax.broadcasted_iota(jnp.int32, s.shape, 1) // group[1]
        s = jnp.where(tok_b == key_b, s, -jnp.inf)
    e = jnp.exp(s - jnp.max(s, axis=1, keepdims=True))
    p = (e / jnp.sum(e, axis=1, keepdims=True)).astype(BF16)
    pv = values(p)
    o = jnp.zeros((tt, mw), F32)
    for h in range(N_MEM_HEADS):
        o = o + jnp.where(lane // HEAD_DIM == h, pv[h * tt:(h + 1) * tt], 0.0)
    o_ref[...] = x2 + _dot(o.astype(BF16), wo_ref[...])


def _mix_xattn(x, o_fox, o_ret, w_out, g_cross, w_cq, mk, mv, w_co, *, tt, keys_per_block, mem_index, group):
    t, d = x.shape
    w = GROUP_WIDTH
    mw = w_cq.shape[1]
    row = lambda width: pl.BlockSpec((tt, width), lambda i: (i, 0))
    if mk.ndim == 3:
        mem = pl.BlockSpec((keys_per_block,) + mk.shape[1:], lambda i: (mem_index(i), 0, 0))
    else:
        mem = pl.BlockSpec((keys_per_block, mw), lambda i: (mem_index(i), 0))
    return pl.pallas_call(
        functools.partial(_mix_xattn_kernel, group=group),
        grid=(t // tt,),
        in_specs=[row(d), row(w), row(w), _resident(w_out.shape), _resident((1, d)), _resident(w_cq.shape),
                  mem, mem, _resident(w_co.shape)],
        out_specs=row(d),
        out_shape=jax.ShapeDtypeStruct((t, d), F32),
        compiler_params=_params("parallel"),
        name="mix_cross_attention",
    )(x, o_fox, o_ret, w_out, g_cross, w_cq, mk, mv, w_co)


def _memkv_kernel(m_ref, g_ref, wk_ref, wv_ref, k_ref, v_ref):
    h = _rms(m_ref[...], g_ref[...]).astype(BF16)
    k_ref[...] = _dot(h, wk_ref[...])
    v_ref[...] = _dot(h, wv_ref[...])


def _memkv(mem, g, wk, wv, *, tm):
    t, d = mem.shape
    mw = wk.shape[1]
    row = lambda width: pl.BlockSpec((tm, width), lambda i: (i, 0))
    return pl.pallas_call(
        _memkv_kernel,
        grid=(t // tm,),
        in_specs=[row(d), _resident((1, d)), _resident(wk.shape), _resident(wv.shape)],
        out_specs=[row(mw), row(mw)],
        out_shape=[jax.ShapeDtypeStruct((t, mw), F32)] * 2,
        compiler_params=_params("parallel"),
        name="memory_kv",
    )(mem, g, wk, wv)


def _rope_tables(pos):
    half = HEAD_DIM // 2
    inv = ROPE_THETA ** (-jnp.arange(half, dtype=F32) / half)
    ang = pos.astype(F32)[:, None] * inv[None, :]
    cos = jnp.cos(ang)
    sin = jnp.sin(ang)
    return (jnp.tile(jnp.concatenate([cos, cos], axis=-1), (1, N_HEADS)),
            jnp.tile(jnp.concatenate([-sin, sin], axis=-1), (1, N_HEADS)))


def _token_tile(t):
    return 512 if t % 512 == 0 else t


def kernel(x_prompt, x_sample, mem_prompt, cache_fox_k, cache_fox_v, cache_fox_logf, state_ret, cache_mem_k, cache_mem_v, page_table, g_ffn1, w1_gate, w1_up, w1_down, g_mix, w_in, b_f, g_ret, w_out, g_cross, g_mem, w_cq, w_ck, w_cv, w_co, g_ffn2, w2_gate, w2_up, w2_down, g_final):
    depth = w_in.shape[0]
    assert depth == 1, "single-layer trunk"
    b, s, d = x_prompt.shape
    db, ds, _ = x_sample.shape
    n_mem = mem_prompt.shape[1]
    page = cache_fox_k.shape[2]
    past_len = page_table.shape[1] * page
    w = GROUP_WIDTH
    assert s % RET_CHUNK == 0 and ds <= RET_CHUNK

    row = lambda a: a.reshape(1, -1)
    bf = lambda a: a.astype(BF16)
    wi = w_in[0]
    fox_end = 3 * w
    wfox = bf(wi[:, :fox_end])
    wf = bf(jnp.pad(wi[:, fox_end:fox_end + N_HEADS], ((0, 0), (0, LANES - N_HEADS))))
    bfp = jnp.pad(row(b_f[0]), ((0, 0), (0, LANES - N_HEADS)))
    ret0 = fox_end + N_HEADS
    wq_r, wk_r, wv_r, wg_r = (wi[:, ret0 + i * w:ret0 + (i + 1) * w] for i in range(4))
    lane = jnp.arange(w)
    swap = jnp.where(lane % HEAD_DIM < HEAD_DIM // 2, lane + HEAD_DIM // 2, lane - HEAD_DIM // 2)
    wret = bf(jnp.concatenate([wq_r, wq_r[:, swap], wk_r, wk_r[:, swap], wv_r, wg_r], axis=1))
    w1 = (bf(w1_gate[0]), bf(w1_up[0]), bf(w1_down[0]))
    w2 = (bf(w2_gate[0]), bf(w2_up[0]), bf(w2_down[0]))
    wo, wcq, wck, wcv, wco = bf(w_out[0]), bf(w_cq[0]), bf(w_ck[0]), bf(w_cv[0]), bf(w_co[0])
    gf = row(g_final)

    def trunk_in(x, pos):
        tm = _token_tile(x.shape[0])
        x1 = _ffn(x, row(g_ffn1[0]), *w1, gf, final_norm=False, tm=tm)
        cos, sin = _rope_tables(pos)
        return (x1,) + tuple(_inproj(x1, row(g_mix[0]), wfox, wf, bfp, wret, cos, sin, tm=tm))

    def trunk_out(x1, o_fox, o_ret, mk, mv, *, tt, keys_per_block, mem_index, group):
        tm = _token_tile(x1.shape[0])
        x3 = _mix_xattn(x1, o_fox, o_ret, wo, row(g_cross[0]), wcq, mk, mv, wco, tt=tt,
                        keys_per_block=keys_per_block, mem_index=mem_index, group=group)
        return _ffn(x3, row(g_ffn2[0]), *w2, gf, final_norm=True, tm=tm)

    tq = 256
    tm_p = _token_tile(b * s)
    xp1 = _ffn(x_prompt.reshape(b * s, d), row(g_ffn1[0]), *w1, gf, final_norm=False, tm=tm_p)
    cos_p, sin_p = _rope_tables(jnp.arange(s, dtype=jnp.int32))
    wi_t = wi.T
    qt, kt, vt, kab, vtb, lfp, lft, qr, kr, vr, gr = _inproj_prompt(
        xp1, row(g_mix[0]), bf(wi_t[:fox_end]), bf(wi[:, w:2 * w]), wf, bfp, wret, cos_p, sin_p,
        batch=b, tm=tm_p, tk=tq)
    fb, frow = _cumsum(lfp.reshape(b, s, LANES), lft, tc=512)
    o_fox = _fox_prompt(qt, kab.reshape(b, s, w), fb, vtb, frow, tq=tq, pairs=2)
    st_zero = jnp.zeros((b, w // LANES, LANES, LANES), F32)
    o_ret, st_p = _retention(qr.reshape(b, s, w), kr.reshape(b, s, w), vr.reshape(b, s, w), gr.reshape(b, s, w),
                             row(g_ret[0]), st_zero, _ret_tables(RET_CHUNK), n_chunks=4)
    mk, mv = _memkv(mem_prompt.reshape(b * n_mem, d), row(g_mem[0]), wck, wcv, tm=_token_tile(b * n_mem))
    tt = 512
    y_prompt = trunk_out(xp1, o_fox.reshape(b * s, w), o_ret.reshape(b * s, w), mk, mv, tt=tt,
                         keys_per_block=n_mem, mem_index=lambda i: i // (s // tt), group=None)

    pos_s = past_len + jnp.arange(ds, dtype=jnp.int32)
    xs1, qa_s, ka_s, va_s, _, _, lf_s, qr_s, kr_s, vr_s, gr_s = trunk_in(x_sample.reshape(db * ds, d), jnp.tile(pos_s, db))
    feature_major = lambda c: jnp.transpose(c, (0, 2, 3, 1)).reshape(c.shape[0], w, page)
    new_page = lambda a, width: jnp.pad(jnp.transpose(a.reshape(db, ds, width), (0, 2, 1)),
                                        ((0, 0), (0, 0), (0, page - ds)))
    o_fox_s = _fox_sample(page_table, qa_s.astype(F32).reshape(db, ds, w), ka_s.reshape(db, ds, w),
                          va_s.reshape(db, ds, w), new_page(lf_s, N_HEADS), feature_major(cache_fox_k[0]), feature_major(cache_fox_v[0]),
                          jnp.transpose(cache_fox_logf[0], (0, 2, 1)), pp=min(32, page_table.shape[1]))
    batch_minor = lambda a: jnp.transpose(a.astype(F32).reshape(db, ds, N_HEADS, HEAD_DIM), (1, 2, 3, 0))
    o_ret_t, st_s_t = _ret_sample(batch_minor(qr_s), batch_minor(kr_s), batch_minor(vr_s), batch_minor(gr_s),
                                  g_ret[0].reshape(N_HEADS, HEAD_DIM, 1), _ret_sample_table(ds, db),
                                  jnp.transpose(state_ret[0], (1, 2, 3, 0)))
    o_ret_s = jnp.transpose(o_ret_t, (3, 0, 1, 2)).reshape(db * ds, w)
    bb = 8
    mem_feature_major = lambda c: jnp.transpose(c, (0, 2, 3, 1)).reshape(db, -1, n_mem)
    y_sample = trunk_out(xs1, bf(o_fox_s.reshape(db * ds, w)), o_ret_s,
                         mem_feature_major(cache_mem_k[0]), mem_feature_major(cache_mem_v[0]),
                         tt=bb * ds, keys_per_block=bb, mem_index=lambda i: i, group=(ds, n_mem))

    mem_heads = lambda a: a.reshape(1, b, n_mem, N_MEM_HEADS, -1)
    token_major = lambda a: jnp.transpose(a.reshape(1, b, N_HEADS, HEAD_DIM, s), (0, 1, 4, 2, 3))
    return (y_prompt.reshape(b, s, d), y_sample.reshape(db, ds, d),
            token_major(kt), token_major(vt),
            jnp.transpose(lft, (0, 2, 1))[None], _head_states(st_p)[None], mem_heads(mk), mem_heads(mv),
            ka_s.reshape(1, db, ds, N_HEADS, HEAD_DIM), va_s.reshape(1, db, ds, N_HEADS, HEAD_DIM),
            lf_s.reshape(1, db, ds, N_HEADS), jnp.transpose(st_s_t, (3, 0, 1, 2))[None])
```
